```python
import math
import jax, jax.numpy as jnp
from jax import lax
import numpy as np

D_MODEL = 1024
BATCH = 8
SEQ = 4096
DEPTH = 1
DEC_BATCH = 128
DEC_SEQ = 8
PAST_LEN = 8192
PAGE_SIZE = 128

MEM_LEN = 256
H_FOX = 8
HD_FOX = 64
W_FOX = H_FOX * HD_FOX
H_DIFF = 4
HD_DIFF = 64
W_DIFF = H_DIFF * 2 * HD_DIFF
H_MEM = 4
HD_MEM = D_MODEL // H_MEM
D_FF = ((8 * D_MODEL // 3 + 127) // 128) * 128
CONV_W = 3
Q_BLOCK = 128
ROPE_THETA = 10000.0
FGATE_BIAS = 3.0
EPS = 1e-6
D_IN = 3 * W_FOX + H_FOX + 3 * W_DIFF + 2 * D_MODEL

kernel_name = 'fox_diff_gated_hybrid_step'


def _rmsnorm(x, g):
    xf = x.astype(jnp.float32)
    r = lax.rsqrt(jnp.mean(xf * xf, axis=-1, keepdims=True) + EPS)
    return (xf * r * g.astype(jnp.float32)).astype(x.dtype)


def _rope(x, pos):
    d = x.shape[-1]
    half = d // 2
    inv = ROPE_THETA ** (-jnp.arange(half, dtype=jnp.float32) * (2.0 / d))
    ang = pos.astype(jnp.float32)[:, None] * inv[None, :]
    shape = (1, pos.shape[0]) + (1,) * (x.ndim - 3) + (half,)
    cos = jnp.cos(ang).reshape(shape)
    sin = jnp.sin(ang).reshape(shape)
    xf = x.astype(jnp.float32)
    x1, x2 = xf[..., :half], xf[..., half:]
    return jnp.concatenate([x1 * cos - x2 * sin, x2 * cos + x1 * sin], axis=-1).astype(x.dtype)


def _block_sweep(fn, n_q):
    blk = math.gcd(n_q, Q_BLOCK)
    starts = jnp.arange(n_q // blk, dtype=jnp.int32) * blk
    out = lax.map(lambda s: fn(s, blk), starts)
    out = jnp.moveaxis(out, 0, 1)
    return out.reshape((out.shape[0], n_q) + out.shape[3:])


def _fox_attention(q, k, v, c_q, c_k, q_pos, k_pos):
    scale = HD_FOX ** -0.5
    ck = jnp.swapaxes(c_k, 1, 2)[:, :, None, :]

    def blockfn(s, blk):
        qb = lax.dynamic_slice_in_dim(q, s, blk, 1)
        cb = jnp.swapaxes(lax.dynamic_slice_in_dim(c_q, s, blk, 1), 1, 2)[:, :, :, None]
        pb = lax.dynamic_slice_in_dim(q_pos, s, blk, 0)
        logits = jnp.einsum('bqhd,bkhd->bhqk', qb, k, preferred_element_type=jnp.float32) * scale
        logits = logits + (cb - ck)
        mask = k_pos[None, :] <= pb[:, None]
        logits = jnp.where(mask, logits, -jnp.inf)
        p = jax.nn.softmax(logits, axis=-1)
        return jnp.einsum('bhqk,bkhd->bqhd', p.astype(v.dtype), v)

    return _block_sweep(blockfn, q.shape[1])


def _diff_attention(q, k, v, q_pos, k_pos, lam):
    scale = HD_DIFF ** -0.5

    def blockfn(s, blk):
        qb = lax.dynamic_slice_in_dim(q, s, blk, 1)
        pb = lax.dynamic_slice_in_dim(q_pos, s, blk, 0)
        logits = jnp.einsum('bqhmd,bkhmd->bmhqk', qb, k, preferred_element_type=jnp.float32) * scale
        mask = k_pos[None, :] <= pb[:, None]
        logits = jnp.where(mask, logits, -jnp.inf)
        p = jax.nn.softmax(logits, axis=-1)
        a = p[:, 0] - lam * p[:, 1]
        return jnp.einsum('bhqk,bkhe->bqhe', a.astype(v.dtype), v)

    return _block_sweep(blockfn, q.shape[1])


def _gather_pages(cache, l, page_table):
    g = cache[l, page_table]
    return g.reshape((g.shape[0], g.shape[1] * g.shape[2]) + g.shape[3:])


def _mem_kv(mem, g_mem, w_ck, w_cv):
    mn = _rmsnorm(mem, g_mem)
    b, m = mem.shape[0], mem.shape[1]
    return ((mn @ w_ck).reshape(b, m, H_MEM, HD_MEM), (mn @ w_cv).reshape(b, m, H_MEM, HD_MEM))


def _layer(x, past_fk, past_fv, past_lf, past_dk, past_dv, mem_k, mem_v, conv_prev,
           g_mix, w_in, b_fgate, lq1, lk1, lq2, lk2, g_subln, w_br_fox, w_br_diff, w_mix_out,
           g_cross, w_cq, w_co, g_ffn, w_up, conv_w, conv_b, w_down, lam_init):
    B, L, _ = x.shape
    P = past_fk.shape[1]
    q_pos = P + jnp.arange(L, dtype=jnp.int32)
    k_pos = jnp.arange(P + L, dtype=jnp.int32)

    h = _rmsnorm(x, g_mix)
    z = h @ w_in
    sizes = [W_FOX, W_FOX, W_FOX, H_FOX, W_DIFF, W_DIFF, W_DIFF, D_MODEL]
    cuts = [int(c) for c in np.cumsum(sizes)]
    fq, fk, fv, fl, dq, dk, dv, ga, gb = jnp.split(z, cuts, axis=-1)

    fq = fq.reshape(B, L, H_FOX, HD_FOX)
    fk = fk.reshape(B, L, H_FOX, HD_FOX)
    fv = fv.reshape(B, L, H_FOX, HD_FOX)
    logf = jax.nn.log_sigmoid((fl + b_fgate).astype(jnp.float32))
    fk_all = jnp.concatenate([past_fk, fk.astype(past_fk.dtype)], axis=1)
    fv_all = jnp.concatenate([past_fv, fv.astype(past_fv.dtype)], axis=1)
    cum = jnp.cumsum(jnp.concatenate([past_lf.astype(jnp.float32), logf], axis=1), axis=1)
    o_fox = _fox_attention(fq, fk_all, fv_all, cum[:, P:], cum, q_pos, k_pos).reshape(B, L, W_FOX)

    dq = _rope(dq.reshape(B, L, H_DIFF, 2, HD_DIFF), q_pos)
    dk = _rope(dk.reshape(B, L, H_DIFF, 2, HD_DIFF), q_pos)
    dv = dv.reshape(B, L, H_DIFF, 2 * HD_DIFF)
    dk_all = jnp.concatenate([past_dk, dk.astype(past_dk.dtype)], axis=1)
    dv_all = jnp.concatenate([past_dv, dv.astype(past_dv.dtype)], axis=1)
    f32 = jnp.float32
    lam = (jnp.exp(jnp.dot(lq1.astype(f32), lk1.astype(f32)))
           - jnp.exp(jnp.dot(lq2.astype(f32), lk2.astype(f32))) + lam_init)
    o_diff = _diff_attention(dq, dk_all, dv_all, q_pos, k_pos, lam)
    o_diff = (_rmsnorm(o_diff, g_subln) * (1.0 - lam_init)).reshape(B, L, W_DIFF).astype(x.dtype)

    merged = jax.nn.sigmoid(ga) * (o_fox @ w_br_fox) + jax.nn.sigmoid(gb) * (o_diff @ w_br_diff)
    x = x + merged @ w_mix_out

    hc = _rmsnorm(x, g_cross)
    cq = (hc @ w_cq).reshape(B, L, H_MEM, HD_MEM)
    cl = jnp.einsum('bqhd,bkhd->bhqk', cq, mem_k, preferred_element_type=jnp.float32) * HD_MEM ** -0.5
    cp = jax.nn.softmax(cl, axis=-1)
    co = jnp.einsum('bhqk,bkhd->bqhd', cp.astype(mem_v.dtype), mem_v).reshape(B, L, D_MODEL)
    x = x + co @ w_co

    hf = _rmsnorm(x, g_ffn)
    a, b = jnp.split(hf @ w_up, 2, axis=-1)
    a_ext = jnp.concatenate([conv_prev.astype(a.dtype), a], axis=1)
    a_c = lax.conv_general_dilated(a_ext, conv_w[:, None, :].astype(a.dtype), (1,), 'VALID',
                                   dimension_numbers=('NWC', 'WIO', 'NWC'),
                                   feature_group_count=D_FF) + conv_b
    x = x + (jax.nn.silu(a_c) * b) @ w_down
    conv_new = a_ext[:, a_ext.shape[1] - (CONV_W - 1):]
    return x, fk, fv, logf, dk, dv, conv_new


def setup_inputs(seed: int = 0) -> dict:
    key = jax.random.key(seed)
    it = iter(list(jax.random.split(key, 48)))

    def nrm(shape, scale=1.0):
        return jax.random.normal(next(it), shape, jnp.float32) * scale

    n_pages = PAST_LEN // PAGE_SIZE
    n_used = DEC_BATCH * n_pages
    n_pool = n_used + max(1, n_used // 4)
    d = {}
    d['x_prompt'] = nrm((BATCH, SEQ, D_MODEL))
    d['x_sample'] = nrm((DEC_BATCH, DEC_SEQ, D_MODEL))
    d['cache_fox_k'] = nrm((DEPTH, n_pool, PAGE_SIZE, H_FOX, HD_FOX))
    d['cache_fox_v'] = nrm((DEPTH, n_pool, PAGE_SIZE, H_FOX, HD_FOX))
    d['cache_fox_logf'] = jax.nn.log_sigmoid(FGATE_BIAS + nrm((DEPTH, n_pool, PAGE_SIZE, H_FOX)))
    d['cache_diff_k'] = nrm((DEPTH, n_pool, PAGE_SIZE, H_DIFF, 2, HD_DIFF))
    d['cache_diff_v'] = nrm((DEPTH, n_pool, PAGE_SIZE, H_DIFF, 2 * HD_DIFF))
    d['cache_mem_k'] = nrm((DEPTH, DEC_BATCH, MEM_LEN, H_MEM, HD_MEM))
    d['cache_mem_v'] = nrm((DEPTH, DEC_BATCH, MEM_LEN, H_MEM, HD_MEM))
    d['state_conv'] = nrm((DEPTH, DEC_BATCH, CONV_W - 1, D_FF))
    d['page_table'] = jax.random.permutation(next(it), n_pool)[:n_used].reshape(DEC_BATCH, n_pages).astype(jnp.int32)
    d['mem_prompt'] = nrm((BATCH, MEM_LEN, D_MODEL))
    d['g_mix'] = 1.0 + nrm((DEPTH, D_MODEL), 0.01)
    d['w_in'] = nrm((DEPTH, D_MODEL, D_IN), D_MODEL ** -0.5)
    d['b_fgate'] = FGATE_BIAS + nrm((DEPTH, H_FOX), 0.1)
    d['lambda_q1'] = nrm((DEPTH, HD_DIFF), 0.1)
    d['lambda_k1'] = nrm((DEPTH, HD_DIFF), 0.1)
    d['lambda_q2'] = nrm((DEPTH, HD_DIFF), 0.1)
    d['lambda_k2'] = nrm((DEPTH, HD_DIFF), 0.1)
    d['g_subln'] = 1.0 + nrm((DEPTH, 2 * HD_DIFF), 0.01)
    d['w_br_fox'] = nrm((DEPTH, W_FOX, D_MODEL), W_FOX ** -0.5)
    d['w_br_diff'] = nrm((DEPTH, W_DIFF, D_MODEL), W_DIFF ** -0.5)
    d['w_mix_out'] = nrm((DEPTH, D_MODEL, D_MODEL), D_MODEL ** -0.5)
    d['g_cross'] = 1.0 + nrm((DEPTH, D_MODEL), 0.01)
    d['g_mem'] = 1.0 + nrm((DEPTH, D_MODEL), 0.01)
    d['w_cq'] = nrm((DEPTH, D_MODEL, D_MODEL), D_MODEL ** -0.5)
    d['w_ck'] = nrm((DEPTH, D_MODEL, D_MODEL), D_MODEL ** -0.5)
    d['w_cv'] = nrm((DEPTH, D_MODEL, D_MODEL), D_MODEL ** -0.5)
    d['w_co'] = nrm((DEPTH, D_MODEL, D_MODEL), D_MODEL ** -0.5)
    d['g_ffn'] = 1.0 + nrm((DEPTH, D_MODEL), 0.01)
    d['w_up'] = nrm((DEPTH, D_MODEL, 2 * D_FF), D_MODEL ** -0.5)
    d['conv_w'] = nrm((DEPTH, CONV_W, D_FF), CONV_W ** -0.5)
    d['conv_b'] = nrm((DEPTH, D_FF), 0.01)
    d['w_down'] = nrm((DEPTH, D_FF, D_MODEL), D_FF ** -0.5)
    d['g_final'] = 1.0 + nrm((D_MODEL,), 0.01)
    return d


def reference(x_prompt, x_sample, cache_fox_k, cache_fox_v, cache_fox_logf, cache_diff_k, cache_diff_v,
              cache_mem_k, cache_mem_v, state_conv, page_table, mem_prompt,
              g_mix, w_in, b_fgate, lambda_q1, lambda_k1, lambda_q2, lambda_k2, g_subln,
              w_br_fox, w_br_diff, w_mix_out, g_cross, g_mem, w_cq, w_ck, w_cv, w_co,
              g_ffn, w_up, conv_w, conv_b, w_down, g_final):
    bp = x_prompt.shape[0]
    xp = x_prompt
    xs = x_sample
    pfk, pfv, plf, pdk, pdv, pmk, pmv, pcv = [], [], [], [], [], [], [], []
    sfk, sfv, slf, sdk, sdv, scv = [], [], [], [], [], []
    for l in range(DEPTH):
        lam_init = 0.8 - 0.6 * math.exp(-0.3 * l)
        wl = (g_mix[l], w_in[l], b_fgate[l], lambda_q1[l], lambda_k1[l], lambda_q2[l], lambda_k2[l],
              g_subln[l], w_br_fox[l], w_br_diff[l], w_mix_out[l], g_cross[l], w_cq[l], w_co[l],
              g_ffn[l], w_up[l], conv_w[l], conv_b[l], w_down[l], lam_init)

        mk, mv = _mem_kv(mem_prompt, g_mem[l], w_ck[l], w_cv[l])
        dt = xp.dtype
        xp, fk, fv, lf, dk, dv, cv = _layer(
            xp,
            jnp.zeros((bp, 0, H_FOX, HD_FOX), dt), jnp.zeros((bp, 0, H_FOX, HD_FOX), dt),
            jnp.zeros((bp, 0, H_FOX), jnp.float32),
            jnp.zeros((bp, 0, H_DIFF, 2, HD_DIFF), dt), jnp.zeros((bp, 0, H_DIFF, 2 * HD_DIFF), dt),
            mk, mv, jnp.zeros((bp, CONV_W - 1, D_FF), dt), *wl)
        pfk.append(fk); pfv.append(fv); plf.append(lf); pdk.append(dk); pdv.append(dv)
        pmk.append(mk); pmv.append(mv); pcv.append(cv)

        xs, fk, fv, lf, dk, dv, cv = _layer(
            xs,
            _gather_pages(cache_fox_k, l, page_table), _gather_pages(cache_fox_v, l, page_table),
            _gather_pages(cache_fox_logf, l, page_table),
            _gather_pages(cache_diff_k, l, page_table), _gather_pages(cache_diff_v, l, page_table),
            cache_mem_k[l], cache_mem_v[l], state_conv[l], *wl)
        sfk.append(fk); sfv.append(fv); slf.append(lf); sdk.append(dk); sdv.append(dv); scv.append(cv)

    y_prompt = _rmsnorm(xp, g_final)
    y_sample = _rmsnorm(xs, g_final)
    return (y_prompt, y_sample,
            jnp.stack(pfk), jnp.stack(pfv), jnp.stack(plf), jnp.stack(pdk), jnp.stack(pdv),
            jnp.stack(pmk), jnp.stack(pmv), jnp.stack(pcv),
            jnp.stack(sfk), jnp.stack(sfv), jnp.stack(slf), jnp.stack(sdk), jnp.stack(sdv), jnp.stack(scv))
```

```python
import functools
import math

import jax
import jax.numpy as jnp
from jax import lax
from jax.experimental import pallas as pl
from jax.experimental.pallas import tpu as pltpu

F32 = jnp.float32
BF16 = jnp.bfloat16

D_MODEL = 1024
H_FOX = 8
HD_FOX = 64
W_FOX = H_FOX * HD_FOX
H_DIFF = 4
HD_DIFF = 64
W_DIFF = H_DIFF * 2 * HD_DIFF
H_MEM = 4
HD_MEM = D_MODEL // H_MEM
D_FF = 2816
CONV_W = 3
PAGE_SIZE = 128
ROPE_THETA = 10000.0
EPS = 1e-6
LAM_INIT = 0.8 - 0.6 * math.exp(-0.3 * 0)

LANES = 128
V7X_VMEM_BYTES = 64 * 1024 * 1024
VMEM_LIMIT = V7X_VMEM_BYTES - 8 * 1024 * 1024
NEG_INF = float("-inf")

NT_DIMS = (((1,), (1,)), ((), ()))
TN_DIMS = (((0,), (0,)), ((), ()))


def _rms(x, g):
    r = lax.rsqrt(jnp.mean(x * x, axis=-1, keepdims=True) + EPS)
    return x * r * g


def _resident(shape):
    nd = len(shape)
    return pl.BlockSpec(shape, lambda *_: (0,) * nd, pipeline_mode=pl.Buffered(1))


def _params(semantics):
    return pltpu.CompilerParams(dimension_semantics=semantics, vmem_limit_bytes=VMEM_LIMIT)


def _rope(x, cos, sin_signed):
    n = x.shape[1]
    lane = lax.broadcasted_iota(jnp.int32, x.shape, 1)
    partner = jnp.where(lane % HD_DIFF < HD_DIFF // 2,
                        pltpu.roll(x, n - HD_DIFF // 2, 1),
                        pltpu.roll(x, HD_DIFF // 2, 1))
    reps = n // LANES
    c = jnp.concatenate([cos] * reps, axis=1)
    s = jnp.concatenate([sin_signed] * reps, axis=1)
    return x * c + partner * s


def _in_proj_kernel(x_ref, g_ref, w_ref, wfl_ref, bfl_ref, cos_ref, sin_ref,
                    fq_ref, fk_ref, fv_ref, fkb_ref, fvb_ref, lf_ref,
                    dq_ref, dk_ref, dv_ref, dkb_ref, dvb_ref, ga_ref, gb_ref,
                    *rest, tiles_per_seq):
    h = _rms(x_ref[...], g_ref[...]).astype(BF16)

    def proj(c0, n):
        return jnp.dot(h, w_ref[:, c0:c0 + n], preferred_element_type=F32)

    fq_ref[...] = (proj(0, W_FOX) * HD_FOX ** -0.5).astype(BF16)
    fk = proj(W_FOX, W_FOX)
    fk_ref[...] = fk
    fkb_ref[...] = fk.astype(BF16)
    fv = proj(2 * W_FOX, W_FOX)
    fv_ref[...] = fv
    fvb_ref[...] = fv.astype(BF16)

    u = jnp.dot(h, wfl_ref[...], preferred_element_type=F32) + bfl_ref[...]
    lf = jnp.minimum(u, 0.0) - jnp.log1p(jnp.exp(-jnp.abs(u)))
    lf_ref[...] = lf[:, :H_FOX]

    cos = cos_ref[...]
    sin = sin_ref[...]
    c0 = 3 * W_FOX
    dq_ref[...] = (_rope(proj(c0, W_DIFF), cos, sin) * HD_DIFF ** -0.5).astype(BF16)
    dk = _rope(proj(c0 + W_DIFF, W_DIFF), cos, sin)
    dk_ref[...] = dk
    dkb_ref[...] = dk.astype(BF16)
    dv = proj(c0 + 2 * W_DIFF, W_DIFF)
    dv_ref[...] = dv
    dvb_ref[...] = dv.astype(BF16)
    c0 += 3 * W_DIFF
    ga_ref[...] = jax.nn.sigmoid(proj(c0, D_MODEL)).astype(BF16)
    gb_ref[...] = jax.nn.sigmoid(proj(c0 + D_MODEL, D_MODEL)).astype(BF16)

    if tiles_per_seq is not None:
        ck_ref, carry_ref = rest
        tm = lf.shape[0]

        @pl.when(pl.program_id(0) % tiles_per_seq == 0)
        def _():
            carry_ref[...] = jnp.zeros_like(carry_ref)

        row = lax.broadcasted_iota(jnp.int32, (tm, tm), 0)
        col = lax.broadcasted_iota(jnp.int32, (tm, tm), 1)
        tri = (row >= col).astype(F32)
        cum = jnp.dot(tri, lf, precision=lax.Precision.HIGHEST,
                      preferred_element_type=F32) + carry_ref[...]
        carry_ref[...] = cum[tm - 1:tm, :]
        ck_ref[...] = cum.T[:H_FOX, :]


def _in_proj(x, g, w_cat, w_fl, b_fl, cos, sin, *, tm, tiles_per_seq):
    t = x.shape[0]
    n_tiles = t // tm
    with_cum = tiles_per_seq is not None
    table_tiles = cos.shape[0] // tm
    row = lambda i: (i, 0)
    tab = lambda i: (i % table_tiles, 0)
    wide = lambda n, dt: jax.ShapeDtypeStruct((t, n), dt)
    blk = lambda n: pl.BlockSpec((tm, n), row)
    out_shape = [wide(W_FOX, BF16), wide(W_FOX, F32), wide(W_FOX, F32), wide(W_FOX, BF16),
                 wide(W_FOX, BF16), wide(H_FOX, F32),
                 wide(W_DIFF, BF16), wide(W_DIFF, F32), wide(W_DIFF, F32), wide(W_DIFF, BF16),
                 wide(W_DIFF, BF16), wide(D_MODEL, BF16), wide(D_MODEL, BF16)]
    out_specs = [blk(W_FOX)] * 5 + [blk(H_FOX)] + [blk(W_DIFF)] * 5 + [blk(D_MODEL)] * 2
    scratch = []
    if with_cum:
        out_shape.append(jax.ShapeDtypeStruct((H_FOX, t), F32))
        out_specs.append(pl.BlockSpec((H_FOX, tm), lambda i: (0, i)))
        scratch.append(pltpu.VMEM((1, LANES), F32))
    return pl.pallas_call(
        functools.partial(_in_proj_kernel, tiles_per_seq=tiles_per_seq),
        grid=(n_tiles,),
        in_specs=[blk(D_MODEL), _resident(g.shape), _resident(w_cat.shape),
                  _resident(w_fl.shape), _resident(b_fl.shape),
                  pl.BlockSpec((tm, LANES), tab), pl.BlockSpec((tm, LANES), tab)],
        out_specs=out_specs,
        out_shape=out_shape,
        scratch_shapes=scratch,
        compiler_params=_params(("arbitrary",)),
        name="in_proj_cum" if with_cum else "in_proj",
    )(x, g, w_cat, w_fl, b_fl, cos, sin)


def _lambda(lam_ref):
    lp = lam_ref[...]
    d1 = jnp.sum(lp[0:1] * lp[1:2], axis=1, keepdims=True)
    d2 = jnp.sum(lp[2:3] * lp[3:4], axis=1, keepdims=True)
    return jnp.exp(d1) - jnp.exp(d2) + LAM_INIT


def _prompt_attn_kernel(qt_ref, kt_ref, q_ref, k_ref, v_ref, *rest, fox):
    if fox:
        ck_ref, o_ref, m_ref, l_ref, acc_ref = rest
    else:
        lam_ref, gs_ref, o_ref, m_ref, l_ref, acc_ref = rest
    t = pl.program_id(2)
    qi = qt_ref[t]
    ki = kt_ref[t]
    tq = q_ref.shape[0]
    tk = k_ref.shape[0]

    @pl.when(ki == 0)
    def _():
        m_ref[...] = jnp.full_like(m_ref, NEG_INF)
        l_ref[...] = jnp.zeros_like(l_ref)
        acc_ref[...] = jnp.zeros_like(acc_ref)

    def step(masked):
        q = q_ref[...]
        k = k_ref[...]
        v = v_ref[...]
        lane = lax.broadcasted_iota(jnp.int32, q.shape, 1)
        if masked:
            row = lax.broadcasted_iota(jnp.int32, (tq, tk), 0)
            col = lax.broadcasted_iota(jnp.int32, (tq, tk), 1)
            visible = col <= row
        for half in range(2):
            sel = (lane < HD_FOX) if half == 0 else (lane >= HD_FOX)
            qh = jnp.where(sel, q, jnp.zeros_like(q))
            s = lax.dot_general(qh, k, NT_DIMS, preferred_element_type=F32)
            if fox:
                s = s - ck_ref[0, half:half + 1, :]
            if masked:
                s = jnp.where(visible, s, NEG_INF)
            m_prev = m_ref[half]
            m_new = jnp.maximum(m_prev, jnp.max(s, axis=1, keepdims=True))
            alpha = jnp.exp(m_prev - m_new)
            p = jnp.exp(s - m_new)
            l_ref[half] = alpha * l_ref[half] + jnp.sum(p, axis=1, keepdims=True)
            acc_ref[half] = alpha * acc_ref[half] + jnp.dot(
                p.astype(BF16), v, preferred_element_type=F32)
            m_ref[half] = m_new

    @pl.when(ki < qi)
    def _():
        step(False)

    @pl.when(ki == qi)
    def _():
        step(True)
        o0 = acc_ref[0] / l_ref[0]
        o1 = acc_ref[1] / l_ref[1]
        if fox:
            lane = lax.broadcasted_iota(jnp.int32, o0.shape, 1)
            o_ref[...] = jnp.where(lane < HD_FOX, o0, o1).astype(o_ref.dtype)
        else:
            o = o0 - _lambda(lam_ref) * o1
            o_ref[...] = (_rms(o, gs_ref[...]) * (1.0 - LAM_INIT)).astype(o_ref.dtype)


def _prompt_attn(q, k, v, extra, *, fox, batch, seq, tq):
    nq = seq // tq
    pairs = q.shape[1] // LANES
    qt = jnp.asarray([i for i in range(nq) for _ in range(i + 1)], jnp.int32)
    kt = jnp.asarray([j for i in range(nq) for j in range(i + 1)], jnp.int32)
    qmap = lambda b, p, t, qt, kt: (b * nq + qt[t], p)
    kmap = lambda b, p, t, qt, kt: (b * nq + kt[t], p)
    in_specs = [pl.BlockSpec((tq, LANES), qmap), pl.BlockSpec((tq, LANES), kmap),
                pl.BlockSpec((tq, LANES), kmap)]
    if fox:
        in_specs.append(pl.BlockSpec((1, 2, tq), lambda b, p, t, qt, kt: (p, 0, b * nq + kt[t])))
    else:
        in_specs += [pl.BlockSpec(e.shape, lambda b, p, t, qt, kt: (0, 0)) for e in extra]
    return pl.pallas_call(
        functools.partial(_prompt_attn_kernel, fox=fox),
        grid_spec=pltpu.PrefetchScalarGridSpec(
            num_scalar_prefetch=2,
            grid=(batch, pairs, int(qt.shape[0])),
            in_specs=in_specs,
            out_specs=pl.BlockSpec((tq, LANES), qmap),
            scratch_shapes=[pltpu.VMEM((2, tq, 1), F32), pltpu.VMEM((2, tq, 1), F32),
                            pltpu.VMEM((2, tq, LANES), F32)]),
        out_shape=jax.ShapeDtypeStruct(q.shape, BF16),
        compiler_params=_params(("arbitrary", "arbitrary", "arbitrary")),
        name="prompt_attn_fox" if fox else "prompt_attn_diff",
    )(qt, kt, q, k, v, *extra)


def _page_cum_kernel(pt_ref, *refs, pages):
    lf_refs = refs[:pages]
    new_ref, ck_ref, cknew_ref, carry_ref = refs[pages:]
    c = pl.program_id(1)

    @pl.when(c == 0)
    def _():
        carry_ref[...] = jnp.zeros_like(carry_ref)

    row = lax.broadcasted_iota(jnp.int32, (PAGE_SIZE, PAGE_SIZE), 0)
    col = lax.broadcasted_iota(jnp.int32, (PAGE_SIZE, PAGE_SIZE), 1)
    tri = (row <= col).astype(F32)

    def page_cum(x):
        return lax.dot_general(x, tri, TN_DIMS, precision=lax.Precision.HIGHEST,
                               preferred_element_type=F32)

    carry = carry_ref[...]
    for j in range(pages):
        cj = page_cum(lf_refs[j][0]) + carry
        ck_ref[0, :, j * PAGE_SIZE:(j + 1) * PAGE_SIZE] = cj
        carry = cj[:, PAGE_SIZE - 1:PAGE_SIZE]
    carry_ref[...] = carry

    @pl.when(c == pl.num_programs(1) - 1)
    def _():
        cknew_ref[0] = page_cum(new_ref[0]) + carry


def _page_cum(page_table, cache_lf, lf_new_page, *, pages):
    b, n_pages = page_table.shape
    steps = n_pages // pages
    in_specs = [pl.BlockSpec((1, PAGE_SIZE, H_FOX),
                             functools.partial(lambda b_, c, pt, j: (pt[b_, c * pages + j], 0, 0), j=j))
                for j in range(pages)]
    in_specs.append(pl.BlockSpec((1, PAGE_SIZE, H_FOX), lambda b_, c, pt: (b_, 0, 0)))
    return pl.pallas_call(
        functools.partial(_page_cum_kernel, pages=pages),
        grid_spec=pltpu.PrefetchScalarGridSpec(
            num_scalar_prefetch=1,
            grid=(b, steps),
            in_specs=in_specs,
            out_specs=[pl.BlockSpec((1, H_FOX, pages * PAGE_SIZE), lambda b_, c, pt: (b_, 0, c)),
                       pl.BlockSpec((1, H_FOX, PAGE_SIZE), lambda b_, c, pt: (b_, 0, 0))],
            scratch_shapes=[pltpu.VMEM((H_FOX, 1), F32)]),
        out_shape=[jax.ShapeDtypeStruct((b, H_FOX, n_pages * PAGE_SIZE), F32),
                   jax.ShapeDtypeStruct((b, H_FOX, PAGE_SIZE), F32)],
        compiler_params=_params(("arbitrary", "arbitrary")),
        name="page_cum",
    )(page_table, *([cache_lf] * pages), lf_new_page)


def _block_diag_q(q):
    n_groups = q.shape[1] // HD_FOX
    rows = q.shape[0]
    qq = jnp.concatenate([q] * n_groups, axis=0)
    r = lax.broadcasted_iota(jnp.int32, qq.shape, 0)
    c = lax.broadcasted_iota(jnp.int32, qq.shape, 1)
    return jnp.where(r // rows == c // HD_FOX, qq, jnp.zeros_like(qq))


def _paged_attn_kernel(pt_ref, *refs, fox, pages):
    q_ref = refs[0]
    k_refs = refs[1:1 + pages]
    v_refs = refs[1 + pages:1 + 2 * pages]
    rest = refs[1 + 2 * pages:]
    knew_ref, vnew_ref = rest[:2]
    if fox:
        ck_ref, cknew_ref, o_ref, m_ref, l_ref, acc_ref = rest[2:]
    else:
        lam_ref, gs_ref, o_ref, m_ref, l_ref, acc_ref = rest[2:]
    c = pl.program_id(1)
    n_q = q_ref.shape[1]
    groups = q_ref.shape[2] // HD_FOX

    @pl.when(c == 0)
    def _():
        m_ref[...] = jnp.full_like(m_ref, NEG_INF)
        l_ref[...] = jnp.zeros_like(l_ref)
        acc_ref[...] = jnp.zeros_like(acc_ref)

    qbd = _block_diag_q(q_ref[0])

    def bias(s, ck):
        return jnp.concatenate(
            [s[g * n_q:(g + 1) * n_q] - ck[g:g + 1] for g in range(groups)], axis=0)

    def update(s, vs):
        m_prev = m_ref[...]
        m_new = jnp.maximum(m_prev, jnp.max(s, axis=1, keepdims=True))
        alpha = jnp.exp(m_prev - m_new)
        p = jnp.exp(s - m_new)
        l_ref[...] = alpha * l_ref[...] + jnp.sum(p, axis=1, keepdims=True)
        p = p.astype(BF16)
        pv = None
        for j, vj in enumerate(vs):
            n = vj.shape[0]
            d = jnp.dot(p[:, j * n:(j + 1) * n], vj, preferred_element_type=F32)
            pv = d if pv is None else pv + d
        acc_ref[...] = alpha * acc_ref[...] + pv
        m_ref[...] = m_new

    s = jnp.concatenate(
        [lax.dot_general(qbd, k_refs[j][0].astype(BF16), NT_DIMS, preferred_element_type=F32)
         for j in range(pages)], axis=1)
    if fox:
        s = bias(s, ck_ref[0])
    update(s, [v_refs[j][0].astype(BF16) for j in range(pages)])

    @pl.when(c == pl.num_programs(1) - 1)
    def _():
        pad = jnp.zeros((PAGE_SIZE - n_q, q_ref.shape[2]), BF16)
        kn = jnp.concatenate([knew_ref[0], pad], axis=0)
        vn = jnp.concatenate([vnew_ref[0], pad], axis=0)
        sn = lax.dot_general(qbd, kn, NT_DIMS, preferred_element_type=F32)
        if fox:
            sn = bias(sn, cknew_ref[0])
        r = lax.broadcasted_iota(jnp.int32, sn.shape, 0)
        col = lax.broadcasted_iota(jnp.int32, sn.shape, 1)
        sn = jnp.where(col <= r % n_q, sn, NEG_INF)
        update(sn, [vn])

        o_full = acc_ref[...] / l_ref[...]
        lane = lax.broadcasted_iota(jnp.int32, (n_q, o_full.shape[1]), 1)
        if fox:
            o = jnp.zeros((n_q, o_full.shape[1]), F32)
            for g in range(groups):
                o = o + jnp.where(lane // HD_FOX == g, o_full[g * n_q:(g + 1) * n_q], 0.0)
            o_ref[0] = o.astype(o_ref.dtype)
        else:
            o0 = jnp.zeros((n_q, o_full.shape[1]), F32)
            o1 = jnp.zeros((n_q, o_full.shape[1]), F32)
            for hd in range(H_DIFF):
                in_head = lane // (2 * HD_DIFF) == hd
                o0 = o0 + jnp.where(in_head, o_full[(2 * hd) * n_q:(2 * hd + 1) * n_q], 0.0)
                o1 = o1 + jnp.where(in_head, o_full[(2 * hd + 1) * n_q:(2 * hd + 2) * n_q], 0.0)
            o = o0 - _lambda(lam_ref) * o1
            w = 2 * HD_DIFF
            o = jnp.concatenate(
                [_rms(o[:, hd * w:(hd + 1) * w], gs_ref[...]) for hd in range(H_DIFF)], axis=1)
            o_ref[0] = (o * (1.0 - LAM_INIT)).astype(o_ref.dtype)


def _paged_attn(page_table, q, cache_k, cache_v, k_new, v_new, extra, *, fox, pages):
    b, n_q, width = q.shape
    n_pages = page_table.shape[1]
    steps = n_pages // pages
    page_spec = lambda j: pl.BlockSpec(
        (1, PAGE_SIZE, width), lambda b_, c, pt: (pt[b_, c * pages + j], 0, 0))
    per_b = pl.BlockSpec((1, n_q, width), lambda b_, c, pt: (b_, 0, 0))
    in_specs = [per_b] + [page_spec(j) for j in range(pages)] * 2 + [per_b, per_b]
    if fox:
        in_specs += [pl.BlockSpec((1, H_FOX, pages * PAGE_SIZE), lambda b_, c, pt: (b_, 0, c)),
                     pl.BlockSpec((1, H_FOX, PAGE_SIZE), lambda b_, c, pt: (b_, 0, 0))]
    else:
        in_specs += [pl.BlockSpec(e.shape, lambda b_, c, pt: (0, 0)) for e in extra]
    rows = n_q * (width // HD_FOX)
    return pl.pallas_call(
        functools.partial(_paged_attn_kernel, fox=fox, pages=pages),
        grid_spec=pltpu.PrefetchScalarGridSpec(
            num_scalar_prefetch=1,
            grid=(b, steps),
            in_specs=in_specs,
            out_specs=per_b,
            scratch_shapes=[pltpu.VMEM((rows, 1), F32), pltpu.VMEM((rows, 1), F32),
                            pltpu.VMEM((rows, width), F32)]),
        out_shape=jax.ShapeDtypeStruct(q.shape, BF16),
        compiler_params=_params(("arbitrary", "arbitrary")),
        name="paged_attn_fox" if fox else "paged_attn_diff",
    )(page_table, q, *([cache_k] * pages), *([cache_v] * pages), k_new, v_new, *extra)


def _mem_kv_kernel(m_ref, g_ref, wk_ref, wv_ref, k_ref, v_ref, kb_ref, vb_ref):
    mn = _rms(m_ref[...], g_ref[...]).astype(BF16)
    k = jnp.dot(mn, wk_ref[...], preferred_element_type=F32)
    v = jnp.dot(mn, wv_ref[...], preferred_element_type=F32)
    k_ref[...] = k
    v_ref[...] = v
    kb_ref[...] = k.astype(BF16)
    vb_ref[...] = v.astype(BF16)


def _mem_kv(mem, g, wk, wv, *, tm):
    t = mem.shape[0]
    blk = pl.BlockSpec((tm, D_MODEL), lambda i: (i, 0))
    return pl.pallas_call(
        _mem_kv_kernel,
        grid=(t // tm,),
        in_specs=[blk, _resident(g.shape), _resident(wk.shape), _resident(wv.shape)],
        out_specs=[blk] * 4,
        out_shape=[jax.ShapeDtypeStruct(mem.shape, F32)] * 2 + [jax.ShapeDtypeStruct(mem.shape, BF16)] * 2,
        compiler_params=_params(("arbitrary",)),
        name="mem_kv",
    )(mem, g, wk, wv)


def _merge_kernel(x_ref, of_ref, od_ref, ga_ref, gb_ref, wbf_ref, wbd_ref, wmo_ref,
                  gc_ref, wcq_ref, x1_ref, cq_ref):
    merged = (ga_ref[...].astype(F32) * jnp.dot(of_ref[...], wbf_ref[...], preferred_element_type=F32)
              + gb_ref[...].astype(F32) * jnp.dot(od_ref[...], wbd_ref[...], preferred_element_type=F32))
    x1 = x_ref[...] + jnp.dot(merged.astype(BF16), wmo_ref[...], preferred_element_type=F32)
    x1_ref[...] = x1
    hc = _rms(x1, gc_ref[...]).astype(BF16)
    cq = jnp.dot(hc, wcq_ref[...], preferred_element_type=F32) * HD_MEM ** -0.5
    cq_ref[...] = cq.astype(cq_ref.dtype)


def _merge(x, of, od, ga, gb, wbf, wbd, wmo, gc, wcq, *, tm, cq_dtype):
    t = x.shape[0]
    row = lambda i: (i, 0)
    blk = lambda n: pl.BlockSpec((tm, n), row)
    return pl.pallas_call(
        _merge_kernel,
        grid=(t // tm,),
        in_specs=[blk(D_MODEL), blk(W_FOX), blk(W_DIFF), blk(D_MODEL), blk(D_MODEL),
                  _resident(wbf.shape), _resident(wbd.shape), _resident(wmo.shape),
                  _resident(gc.shape), _resident(wcq.shape)],
        out_specs=[blk(D_MODEL), blk(D_MODEL)],
        out_shape=[jax.ShapeDtypeStruct(x.shape, F32), jax.ShapeDtypeStruct(x.shape, cq_dtype)],
        compiler_params=_params(("arbitrary",)),
        name="merge",
    )(x, of, od, ga, gb, wbf, wbd, wmo, gc, wcq)


def _cross_attn_kernel(q_ref, k_ref, v_ref, o_ref):
    for hd in range(H_MEM):
        sl = slice(hd * HD_MEM, (hd + 1) * HD_MEM)
        k = k_ref[0, :, sl]
        v = v_ref[0, :, sl]
        q = q_ref[:, sl].astype(k.dtype)
        s = lax.dot_general(q, k, NT_DIMS, preferred_element_type=F32)
        p = jnp.exp(s - jnp.max(s, axis=1, keepdims=True))
        p = p / jnp.sum(p, axis=1, keepdims=True)
        o_ref[:, sl] = jnp.dot(p.astype(v.dtype), v, preferred_element_type=F32).astype(o_ref.dtype)


def _cross_attn(q, mem_k, mem_v, *, tq, out_dtype):
    t = q.shape[0]
    b, m, _ = mem_k.shape
    tiles = t // b // tq
    qmap = lambda b_, i: (b_ * tiles + i, 0)
    mmap = lambda b_, i: (b_, 0, 0)
    return pl.pallas_call(
        _cross_attn_kernel,
        grid=(b, tiles),
        in_specs=[pl.BlockSpec((tq, D_MODEL), qmap), pl.BlockSpec((1, m, D_MODEL), mmap),
                  pl.BlockSpec((1, m, D_MODEL), mmap)],
        out_specs=pl.BlockSpec((tq, D_MODEL), qmap),
        out_shape=jax.ShapeDtypeStruct(q.shape, out_dtype),
        compiler_params=_params(("arbitrary", "arbitrary")),
        name="cross_attn",
    )(q, mem_k, mem_v)


FF_CHUNK = D_FF // 2


def _ffn_kernel(x1_ref, co_ref, wco_ref, gf_ref, wup_ref, cw_ref, cb_ref, wdn_ref, gfin_ref,
                *rest, group):
    if group is None:
        y_ref, cn_ref, carry_ref = rest
    else:
        e1_ref, e2_ref, y_ref, cn_ref = rest
    tm = x1_ref.shape[0]
    x2 = x1_ref[...] + jnp.dot(co_ref[...].astype(BF16), wco_ref[...], preferred_element_type=F32)
    hf = _rms(x2, gf_ref[...]).astype(BF16)
    pos = lax.broadcasted_iota(jnp.int32, (tm, 1), 0)
    if group is None:
        @pl.when(pl.program_id(1) == 0)
        def _():
            carry_ref[...] = jnp.zeros_like(carry_ref)
    else:
        pos = pos % group

    acc = jnp.zeros((tm, D_MODEL), F32)
    for c0 in range(0, D_FF, FF_CHUNK):
        cs = slice(c0, c0 + FF_CHUNK)
        a = jnp.dot(hf, wup_ref[:, cs], preferred_element_type=F32)
        b = jnp.dot(hf, wup_ref[:, D_FF + c0:D_FF + c0 + FF_CHUNK], preferred_element_type=F32)
        if group is None:
            h0 = carry_ref[0:1, cs]
            h1 = carry_ref[1:2, cs]
            a_m1 = jnp.where(pos == 0, h1, pltpu.roll(a, 1, 0))
            a_m2 = jnp.where(pos == 0, h0, jnp.where(pos == 1, h1, pltpu.roll(a, 2, 0)))
            tail = a[tm - (CONV_W - 1):, :]
            carry_ref[0:CONV_W - 1, cs] = tail
            cn_ref[0, :, cs] = tail
        else:
            a_m1 = jnp.where(pos == 0, e1_ref[:, cs], pltpu.roll(a, 1, 0))
            a_m2 = jnp.where(pos < 2, e2_ref[:, cs], pltpu.roll(a, 2, 0))
            cn_ref[:, :, cs] = a.reshape(tm // group, group, FF_CHUNK)[:, group - (CONV_W - 1):, :]
        ac = cw_ref[0:1, cs] * a_m2 + cw_ref[1:2, cs] * a_m1 + cw_ref[2:3, cs] * a + cb_ref[:, cs]
        gate = ac * jax.nn.sigmoid(ac) * b
        acc = acc + jnp.dot(gate.astype(BF16), wdn_ref[cs, :], preferred_element_type=F32)
    y_ref[...] = _rms(x2 + acc, gfin_ref[...])


def _ffn(x1, co, wco, gf, wup, cw, cb, wdn, gfin, hist, *, tm, batch, group):
    t = x1.shape[0]
    tiles = t // batch // tm if group is None else t // tm
    if group is None:
        grid = (batch, tiles)
        row = lambda b_, i: (b_ * tiles + i, 0)
        cn_spec = pl.BlockSpec((1, CONV_W - 1, D_FF), lambda b_, i: (b_, 0, 0))
        hist_specs = []
        scratch = [pltpu.VMEM((8, D_FF), F32)]
    else:
        grid = (1, tiles)
        row = lambda b_, i: (i, 0)
        cn_spec = pl.BlockSpec((tm // group, CONV_W - 1, D_FF), lambda b_, i: (i, 0, 0))
        hist_specs = [pl.BlockSpec((tm, D_FF), row)] * 2
        scratch = []
    blk = pl.BlockSpec((tm, D_MODEL), row)
    return pl.pallas_call(
        functools.partial(_ffn_kernel, group=group),
        grid=grid,
        in_specs=[blk, blk, _resident(wco.shape), _resident(gf.shape), _resident(wup.shape),
                  _resident(cw.shape), _resident(cb.shape), _resident(wdn.shape),
                  _resident(gfin.shape)] + hist_specs,
        out_specs=[blk, cn_spec],
        out_shape=[jax.ShapeDtypeStruct(x1.shape, F32),
                   jax.ShapeDtypeStruct((batch, CONV_W - 1, D_FF), F32)],
        scratch_shapes=scratch,
        compiler_params=_params(("arbitrary", "arbitrary")),
        name="ffn_seq" if group is None else "ffn_grouped",
    )(x1, co, wco, gf, wup, cw, cb, wdn, gfin, *hist)


def _rope_tables(pos):
    half = HD_DIFF // 2
    inv = ROPE_THETA ** (-jnp.arange(half, dtype=F32) * (2.0 / HD_DIFF))
    ang = pos.astype(F32)[:, None] * inv[None, :]
    cos = jnp.cos(ang)
    sin = jnp.sin(ang)
    return (jnp.concatenate([cos] * 4, axis=1), jnp.concatenate([-sin, sin] * 2, axis=1))


def kernel(x_prompt, x_sample, cache_fox_k, cache_fox_v, cache_fox_logf, cache_diff_k, cache_diff_v, cache_mem_k, cache_mem_v, state_conv, page_table, mem_prompt, g_mix, w_in, b_fgate, lambda_q1, lambda_k1, lambda_q2, lambda_k2, g_subln, w_br_fox, w_br_diff, w_mix_out, g_cross, g_mem, w_cq, w_ck, w_cv, w_co, g_ffn, w_up, conv_w, conv_b, w_down, g_final):
    bp, seq, _ = x_prompt.shape
    bs, dec, _ = x_sample.shape
    n_pool = cache_fox_k.shape[1]
    n_pages = page_table.shape[1]
    past = n_pages * PAGE_SIZE
    l = 0

    w = w_in[l]
    c_fl = 3 * W_FOX
    w_cat = jnp.concatenate([w[:, :c_fl], w[:, c_fl + H_FOX:]], axis=1).astype(BF16)
    w_fl = jnp.pad(w[:, c_fl:c_fl + H_FOX], ((0, 0), (0, LANES - H_FOX))).astype(BF16)
    b_fl = jnp.pad(b_fgate[l], (0, LANES - H_FOX))[None, :]
    row = lambda v: v[None, :]
    lam = jnp.stack([lambda_q1[l], lambda_k1[l], lambda_q2[l], lambda_k2[l]])
    gs = row(g_subln[l])
    wbf, wbd, wmo = w_br_fox[l].astype(BF16), w_br_diff[l].astype(BF16), w_mix_out[l].astype(BF16)
    wcq, wck, wcv, wco = (w_cq[l].astype(BF16), w_ck[l].astype(BF16), w_cv[l].astype(BF16),
                          w_co[l].astype(BF16))
    wup, wdn = w_up[l].astype(BF16), w_down[l].astype(BF16)

    def tail(x1, co, hist, *, tm, batch, group):
        return _ffn(x1, co, wco, row(g_ffn[l]), wup, conv_w[l], row(conv_b[l]), wdn, row(g_final),
                    hist, tm=tm, batch=batch, group=group)

    tp = bp * seq
    xp = x_prompt.reshape(tp, D_MODEL)
    cos_p, sin_p = _rope_tables(jnp.arange(seq, dtype=jnp.int32))
    tm_p = 512
    (fq, fk, fv, fkb, fvb, lf, dq, dk, dv, dkb, dvb, ga, gb, ckt) = _in_proj(
        xp, row(g_mix[l]), w_cat, w_fl, b_fl, cos_p, sin_p, tm=tm_p, tiles_per_seq=seq // tm_p)
    ck3 = ckt.reshape(H_FOX // 2, 2, tp)
    o_fox = _prompt_attn(fq, fkb, fvb, (ck3,), fox=True, batch=bp, seq=seq, tq=512)
    o_diff = _prompt_attn(dq, dkb, dvb, (lam, gs), fox=False, batch=bp, seq=seq, tq=512)

    mem_len = mem_prompt.shape[1]
    mk, mv, mkb, mvb = _mem_kv(mem_prompt.reshape(bp * mem_len, D_MODEL), row(g_mem[l]), wck, wcv, tm=256)
    x1, cq = _merge(xp, o_fox, o_diff, ga, gb, wbf, wbd, wmo, row(g_cross[l]), wcq, tm=512, cq_dtype=BF16)
    co = _cross_attn(cq, mkb.reshape(bp, mem_len, D_MODEL), mvb.reshape(bp, mem_len, D_MODEL), tq=512,
                     out_dtype=BF16)
    y_p, p_conv = tail(x1, co, (), tm=256, batch=bp, group=None)

    ts = bs * dec
    xs = x_sample.reshape(ts, D_MODEL)
    tm_s = 256
    pos_s = past + (jnp.arange(tm_s, dtype=jnp.int32) % dec)
    cos_s, sin_s = _rope_tables(pos_s)
    (sfq, sfk, sfv, sfkb, sfvb, slf, sdq, sdk, sdv, sdkb, sdvb, sga, sgb) = _in_proj(
        xs, row(g_mix[l]), w_cat, w_fl, b_fl, cos_s, sin_s, tm=tm_s, tiles_per_seq=None)

    lf_new_page = jnp.pad(slf.reshape(bs, dec, H_FOX), ((0, 0), (0, PAGE_SIZE - dec), (0, 0)))
    ck_past, ck_new = _page_cum(page_table, cache_fox_logf[l], lf_new_page, pages=16)
    b3 = lambda a: a.reshape(bs, dec, a.shape[-1])
    pool = lambda c: c[l].reshape(n_pool, PAGE_SIZE, -1)
    so_fox = _paged_attn(page_table, b3(sfq), pool(cache_fox_k), pool(cache_fox_v), b3(sfkb), b3(sfvb),
                         (ck_past, ck_new), fox=True, pages=8)
    so_diff = _paged_attn(page_table, b3(sdq), pool(cache_diff_k), pool(cache_diff_v), b3(sdkb), b3(sdvb),
                          (lam, gs), fox=False, pages=8)

    sx1, scq = _merge(xs, so_fox.reshape(ts, W_FOX), so_diff.reshape(ts, W_DIFF), sga, sgb, wbf, wbd, wmo,
                      row(g_cross[l]), wcq, tm=tm_s, cq_dtype=F32)
    smem = lambda c: c[l].reshape(bs, c.shape[2], D_MODEL)
    sco = _cross_attn(scq, smem(cache_mem_k), smem(cache_mem_v), tq=dec, out_dtype=F32)
    st = state_conv[l]
    e2 = jnp.pad(st, ((0, 0), (0, dec - 2), (0, 0))).reshape(ts, D_FF)
    e1 = jnp.pad(st[:, 1:], ((0, 0), (0, dec - 1), (0, 0))).reshape(ts, D_FF)
    y_s, s_conv = tail(sx1, sco, (e1, e2), tm=tm_s, batch=bs, group=dec)

    d1 = lambda a, *shape: a.reshape((1,) + shape)
    return (y_p.reshape(bp, seq, D_MODEL), y_s.reshape(bs, dec, D_MODEL),
            d1(fk, bp, seq, H_FOX, HD_FOX), d1(fv, bp, seq, H_FOX, HD_FOX), d1(lf, bp, seq, H_FOX),
            d1(dk, bp, seq, H_DIFF, 2, HD_DIFF), d1(dv, bp, seq, H_DIFF, 2 * HD_DIFF),
            d1(mk, bp, mem_len, H_MEM, HD_MEM), d1(mv, bp, mem_len, H_MEM, HD_MEM),
            d1(p_conv, bp, CONV_W - 1, D_FF),
            d1(sfk, bs, dec, H_FOX, HD_FOX), d1(sfv, bs, dec, H_FOX, HD_FOX), d1(slf, bs, dec, H_FOX),
            d1(sdk, bs, dec, H_DIFF, 2, HD_DIFF), d1(sdv, bs, dec, H_DIFF, 2 * HD_DIFF),
            d1(s_conv, bs, CONV_W - 1, D_FF))
```

```python
import functools
import math

import jax
import jax.numpy as jnp
from jax import lax
from jax.experimental import pallas as pl
from jax.experimental.pallas import tpu as pltpu

F32 = jnp.float32
BF16 = jnp.bfloat16

D_MODEL = 1024
H_FOX = 8
HD_FOX = 64
W_FOX = H_FOX * HD_FOX
H_DIFF = 4
HD_DIFF = 64
W_DIFF = H_DIFF * 2 * HD_DIFF
H_MEM = 4
HD_MEM = D_MODEL // H_MEM
D_FF = 2816
CONV_W = 3
PAGE_SIZE = 128
ROPE_THETA = 10000.0
EPS = 1e-6
LAM_INIT = 0.8 - 0.6 * math.exp(-0.3 * 0)

LANES = 128
V7X_VMEM_BYTES = 64 * 1024 * 1024
VMEM_LIMIT = V7X_VMEM_BYTES - 8 * 1024 * 1024
NEG_INF = float("-inf")

NT_DIMS = (((1,), (1,)), ((), ()))
TN_DIMS = (((0,), (0,)), ((), ()))


def _rms(x, g):
    r = lax.rsqrt(jnp.mean(x * x, axis=-1, keepdims=True) + EPS)
    return x * r * g


def _resident(shape):
    nd = len(shape)
    return pl.BlockSpec(shape, lambda *_: (0,) * nd, pipeline_mode=pl.Buffered(1))


def _params(semantics):
    return pltpu.CompilerParams(dimension_semantics=semantics, vmem_limit_bytes=VMEM_LIMIT)


def _rope(x, cos, sin_signed):
    n = x.shape[1]
    lane = lax.broadcasted_iota(jnp.int32, x.shape, 1)
    partner = jnp.where(lane % HD_DIFF < HD_DIFF // 2,
                        pltpu.roll(x, n - HD_DIFF // 2, 1),
                        pltpu.roll(x, HD_DIFF // 2, 1))
    reps = n // LANES
    c = jnp.concatenate([cos] * reps, axis=1)
    s = jnp.concatenate([sin_signed] * reps, axis=1)
    return x * c + partner * s


def _in_proj_kernel(x_ref, g_ref, w_ref, wfl_ref, bfl_ref, cos_ref, sin_ref,
                    fq_ref, fk_ref, fv_ref, fkb_ref, fvb_ref, lf_ref,
                    dq_ref, dk_ref, dv_ref, dkb_ref, dvb_ref, ga_ref, gb_ref,
                    *rest, tiles_per_seq):
    h = _rms(x_ref[...], g_ref[...]).astype(BF16)

    def proj(c0, n):
        return jnp.dot(h, w_ref[:, c0:c0 + n], preferred_element_type=F32)

    fq_ref[...] = (proj(0, W_FOX) * HD_FOX ** -0.5).astype(BF16)
    fk = proj(W_FOX, W_FOX)
    fk_ref[...] = fk
    fkb_ref[...] = fk.astype(BF16)
    fv = proj(2 * W_FOX, W_FOX)
    fv_ref[...] = fv
    fvb_ref[...] = fv.astype(BF16)

    u = jnp.dot(h, wfl_ref[...], preferred_element_type=F32) + bfl_ref[...]
    lf = jnp.minimum(u, 0.0) - jnp.log1p(jnp.exp(-jnp.abs(u)))
    lf_ref[...] = lf[:, :H_FOX]

    cos = cos_ref[...]
    sin = sin_ref[...]
    c0 = 3 * W_FOX
    dq_ref[...] = (_rope(proj(c0, W_DIFF), cos, sin) * HD_DIFF ** -0.5).astype(BF16)
    dk = _rope(proj(c0 + W_DIFF, W_DIFF), cos, sin)
    dk_ref[...] = dk
    dkb_ref[...] = dk.astype(BF16)
    dv = proj(c0 + 2 * W_DIFF, W_DIFF)
    dv_ref[...] = dv
    dvb_ref[...] = dv.astype(BF16)
    c0 += 3 * W_DIFF
    ga_ref[...] = jax.nn.sigmoid(proj(c0, D_MODEL)).astype(BF16)
    gb_ref[...] = jax.nn.sigmoid(proj(c0 + D_MODEL, D_MODEL)).astype(BF16)

    if tiles_per_seq is not None:
        ck_ref, carry_ref = rest
        tm = lf.shape[0]

        @pl.when(pl.program_id(0) % tiles_per_seq == 0)
        def _():
            carry_ref[...] = jnp.zeros_like(carry_ref)

        row = lax.broadcasted_iota(jnp.int32, (tm, tm), 0)
        col = lax.broadcasted_iota(jnp.int32, (tm, tm), 1)
        tri = (row >= col).astype(F32)
        cum = jnp.dot(tri, lf, precision=lax.Precision.HIGHEST,
                      preferred_element_type=F32) + carry_ref[...]
        carry_ref[...] = cum[tm - 1:tm, :]
        ck_ref[...] = cum.T[:H_FOX, :]


def _in_proj(x, g, w_cat, w_fl, b_fl, cos, sin, *, tm, tiles_per_seq):
    t = x.shape[0]
    n_tiles = t // tm
    with_cum = tiles_per_seq is not None
    table_tiles = cos.shape[0] // tm
    row = lambda i: (i, 0)
    tab = lambda i: (i % table_tiles, 0)
    wide = lambda n, dt: jax.ShapeDtypeStruct((t, n), dt)
    blk = lambda n: pl.BlockSpec((tm, n), row)
    out_shape = [wide(W_FOX, BF16), wide(W_FOX, F32), wide(W_FOX, F32), wide(W_FOX, BF16),
                 wide(W_FOX, BF16), wide(H_FOX, F32),
                 wide(W_DIFF, BF16), wide(W_DIFF, F32), wide(W_DIFF, F32), wide(W_DIFF, BF16),
                 wide(W_DIFF, BF16), wide(D_MODEL, BF16), wide(D_MODEL, BF16)]
    out_specs = [blk(W_FOX)] * 5 + [blk(H_FOX)] + [blk(W_DIFF)] * 5 + [blk(D_MODEL)] * 2
    scratch = []
    if with_cum:
        out_shape.append(jax.ShapeDtypeStruct((H_FOX, t), F32))
        out_specs.append(pl.BlockSpec((H_FOX, tm), lambda i: (0, i)))
        scratch.append(pltpu.VMEM((1, LANES), F32))
    return pl.pallas_call(
        functools.partial(_in_proj_kernel, tiles_per_seq=tiles_per_seq),
        grid=(n_tiles,),
        in_specs=[blk(D_MODEL), _resident(g.shape), _resident(w_cat.shape),
                  _resident(w_fl.shape), _resident(b_fl.shape),
                  pl.BlockSpec((tm, LANES), tab), pl.BlockSpec((tm, LANES), tab)],
        out_specs=out_specs,
        out_shape=out_shape,
        scratch_shapes=scratch,
        compiler_params=_params(("arbitrary",)),
        name="in_proj_cum" if with_cum else "in_proj",
    )(x, g, w_cat, w_fl, b_fl, cos, sin)


def _lambda(lam_ref):
    lp = lam_ref[...]
    d1 = jnp.sum(lp[0:1] * lp[1:2], axis=1, keepdims=True)
    d2 = jnp.sum(lp[2:3] * lp[3:4], axis=1, keepdims=True)
    return jnp.exp(d1) - jnp.exp(d2) + LAM_INIT


def _prompt_attn_kernel(qt_ref, kt_ref, last_ref, q_ref, k_ref, v_ref, *rest, fox):
    if fox:
        ck_ref, o_ref, m_ref, l_ref, acc_ref = rest
    else:
        lam_ref, gs_ref, o_ref, m_ref, l_ref, acc_ref = rest
    t = pl.program_id(1)
    qi = qt_ref[t]
    ki = kt_ref[t]
    tq = q_ref.shape[0]
    tk = k_ref.shape[0]
    pairs = q_ref.shape[1] // LANES
    low = lax.broadcasted_iota(jnp.int32, (tq, LANES), 1) < HD_FOX

    @pl.when(ki == 0)
    def _():
        m_ref[...] = jnp.full_like(m_ref, NEG_INF)
        l_ref[...] = jnp.zeros_like(l_ref)
        acc_ref[...] = jnp.zeros_like(acc_ref)

    def step(masked):
        if masked:
            row = lax.broadcasted_iota(jnp.int32, (tq, tk), 0) + qi * tq
            col = lax.broadcasted_iota(jnp.int32, (tq, tk), 1) + ki * tk
            visible = col <= row
        logits = []
        for pr in range(pairs):
            cols = slice(pr * LANES, (pr + 1) * LANES)
            q = q_ref[:, cols]
            k = k_ref[:, cols]
            for half in range(2):
                qh = jnp.where(low if half == 0 else jnp.logical_not(low), q, jnp.zeros_like(q))
                logits.append(lax.dot_general(qh, k, NT_DIMS, preferred_element_type=F32))
        probs = []
        for idx, s in enumerate(logits):
            if fox:
                s = s - ck_ref[idx:idx + 1, :]
            if masked:
                s = jnp.where(visible, s, NEG_INF)
            m_prev = m_ref[idx]
            m_new = jnp.maximum(m_prev, jnp.max(s, axis=1, keepdims=True))
            alpha = jnp.exp(m_prev - m_new)
            p = jnp.exp(s - m_new)
            l_ref[idx] = alpha * l_ref[idx] + jnp.sum(p, axis=1, keepdims=True)
            m_ref[idx] = m_new
            probs.append((alpha, p.astype(BF16)))
        for idx, (alpha, p) in enumerate(probs):
            pr = idx // 2
            v = v_ref[:, pr * LANES:(pr + 1) * LANES]
            acc_ref[idx] = alpha * acc_ref[idx] + jnp.dot(p, v, preferred_element_type=F32)

    @pl.when(last_ref[t] == 0)
    def _():
        step(False)

    @pl.when(last_ref[t] == 1)
    def _():
        step(True)
        lam = None if fox else _lambda(lam_ref)
        for pr in range(pairs):
            cols = slice(pr * LANES, (pr + 1) * LANES)
            o0 = acc_ref[2 * pr] / l_ref[2 * pr]
            o1 = acc_ref[2 * pr + 1] / l_ref[2 * pr + 1]
            if fox:
                o_ref[:, cols] = jnp.where(low, o0, o1).astype(o_ref.dtype)
            else:
                o = o0 - lam * o1
                o_ref[:, cols] = (_rms(o, gs_ref[...]) * (1.0 - LAM_INIT)).astype(o_ref.dtype)


def _prompt_attn(q, k, v, extra, *, fox, batch, seq, tq, tk):
    nq = seq // tq
    nk = seq // tk
    width = q.shape[1]
    maps = width // HD_FOX
    sched = [(i, j) for i in range(nq) for j in range((i * tq) // tk + 1)]
    qt = jnp.asarray([i for i, _ in sched], jnp.int32)
    kt = jnp.asarray([j for _, j in sched], jnp.int32)
    last = jnp.asarray([int(j == (i * tq) // tk) for i, j in sched], jnp.int32)
    qmap = lambda b, t, qt, kt, last: (b * nq + qt[t], 0)
    kmap = lambda b, t, qt, kt, last: (b * nk + kt[t], 0)
    in_specs = [pl.BlockSpec((tq, width), qmap), pl.BlockSpec((tk, width), kmap),
                pl.BlockSpec((tk, width), kmap)]
    if fox:
        in_specs.append(pl.BlockSpec((maps, tk), lambda b, t, qt, kt, last: (0, b * nk + kt[t])))
    else:
        in_specs += [pl.BlockSpec(e.shape, lambda b, t, qt, kt, last: (0, 0)) for e in extra]
    return pl.pallas_call(
        functools.partial(_prompt_attn_kernel, fox=fox),
        grid_spec=pltpu.PrefetchScalarGridSpec(
            num_scalar_prefetch=3,
            grid=(batch, len(sched)),
            in_specs=in_specs,
            out_specs=pl.BlockSpec((tq, width), qmap),
            scratch_shapes=[pltpu.VMEM((maps, tq, 1), F32), pltpu.VMEM((maps, tq, 1), F32),
                            pltpu.VMEM((maps, tq, LANES), F32)]),
        out_shape=jax.ShapeDtypeStruct(q.shape, BF16),
        compiler_params=_params(("arbitrary", "arbitrary")),
        name="prompt_attn_fox" if fox else "prompt_attn_diff",
    )(qt, kt, last, q, k, v, *extra)


def _page_cum_kernel(pt_ref, *refs, pages):
    lf_refs = refs[:pages]
    new_ref, ck_ref, cknew_ref, carry_ref = refs[pages:]
    c = pl.program_id(1)

    @pl.when(c == 0)
    def _():
        carry_ref[...] = jnp.zeros_like(carry_ref)

    row = lax.broadcasted_iota(jnp.int32, (PAGE_SIZE, PAGE_SIZE), 0)
    col = lax.broadcasted_iota(jnp.int32, (PAGE_SIZE, PAGE_SIZE), 1)
    tri = (row <= col).astype(F32)

    def page_cum(x):
        return jnp.dot(x, tri, precision=lax.Precision.HIGHEST, preferred_element_type=F32)

    carry = carry_ref[...]
    for j in range(pages):
        cj = page_cum(lf_refs[j][0]) + carry
        ck_ref[0, :, j * PAGE_SIZE:(j + 1) * PAGE_SIZE] = cj
        carry = cj[:, PAGE_SIZE - 1:PAGE_SIZE]
    carry_ref[...] = carry

    @pl.when(c == pl.num_programs(1) - 1)
    def _():
        cknew_ref[0] = page_cum(new_ref[0]) + carry


def _page_cum(page_table, cache_lf, lf_new_page, *, pages):
    b, n_pages = page_table.shape
    steps = n_pages // pages
    in_specs = [pl.BlockSpec((1, H_FOX, PAGE_SIZE),
                             functools.partial(lambda b_, c, pt, j: (pt[b_, c * pages + j], 0, 0), j=j))
                for j in range(pages)]
    in_specs.append(pl.BlockSpec((1, H_FOX, PAGE_SIZE), lambda b_, c, pt: (b_, 0, 0)))
    return pl.pallas_call(
        functools.partial(_page_cum_kernel, pages=pages),
        grid_spec=pltpu.PrefetchScalarGridSpec(
            num_scalar_prefetch=1,
            grid=(b, steps),
            in_specs=in_specs,
            out_specs=[pl.BlockSpec((1, H_FOX, pages * PAGE_SIZE), lambda b_, c, pt: (b_, 0, c)),
                       pl.BlockSpec((1, H_FOX, PAGE_SIZE), lambda b_, c, pt: (b_, 0, 0))],
            scratch_shapes=[pltpu.VMEM((H_FOX, 1), F32)]),
        out_shape=[jax.ShapeDtypeStruct((b, H_FOX, n_pages * PAGE_SIZE), F32),
                   jax.ShapeDtypeStruct((b, H_FOX, PAGE_SIZE), F32)],
        compiler_params=_params(("arbitrary", "arbitrary")),
        name="page_cum",
    )(page_table, *([cache_lf] * pages), lf_new_page)


def _block_diag_q(q):
    n_groups = q.shape[1] // HD_FOX
    rows = q.shape[0]
    qq = jnp.concatenate([q] * n_groups, axis=0)
    r = lax.broadcasted_iota(jnp.int32, qq.shape, 0)
    c = lax.broadcasted_iota(jnp.int32, qq.shape, 1)
    return jnp.where(r // rows == c // HD_FOX, qq, jnp.zeros_like(qq))


def _paged_attn_kernel(pt_ref, *refs, fox, pages):
    q_ref = refs[0]
    kt_refs = refs[1:1 + pages]
    v_refs = refs[1 + pages:1 + 2 * pages]
    rest = refs[1 + 2 * pages:]
    knew_ref, vnew_ref = rest[:2]
    if fox:
        ck_ref, cknew_ref, o_ref, m_ref, l_ref, acc_ref = rest[2:]
    else:
        lam_ref, gs_ref, o_ref, m_ref, l_ref, acc_ref = rest[2:]
    c = pl.program_id(1)
    n_q = q_ref.shape[1]
    width = q_ref.shape[2]
    groups = width // HD_FOX
    head_rows = 2 * n_q
    head_w = 2 * HD_DIFF

    @pl.when(c == 0)
    def _():
        m_ref[...] = jnp.full_like(m_ref, NEG_INF)
        l_ref[...] = jnp.zeros_like(l_ref)
        acc_ref[...] = jnp.zeros_like(acc_ref)

    qbd = _block_diag_q(q_ref[0])

    def bias(s, ck):
        return jnp.concatenate(
            [s[g * n_q:(g + 1) * n_q] - ck[g:g + 1] for g in range(groups)], axis=0)

    def softmax_step(s):
        m_prev = m_ref[...]
        m_new = jnp.maximum(m_prev, jnp.max(s, axis=1, keepdims=True))
        alpha = jnp.exp(m_prev - m_new)
        p = jnp.exp(s - m_new)
        l_ref[...] = alpha * l_ref[...] + jnp.sum(p, axis=1, keepdims=True)
        m_ref[...] = m_new
        return alpha, p.astype(BF16)

    def pair(refs_, j, axis, index=()):
        return jnp.concatenate([refs_[j][(0,) + index].astype(BF16),
                                refs_[j + 1][(0,) + index].astype(BF16)], axis=axis)

    def accumulate(alpha, terms):
        for rows, prods in terms:
            pv = None
            for lhs, rhs, dims in prods:
                d = lax.dot_general(lhs, rhs, dims, preferred_element_type=F32)
                pv = d if pv is None else pv + d
            acc_ref[rows, :] = alpha[rows] * acc_ref[rows, :] + pv

    nn = (((1,), (0,)), ((), ()))
    all_rows = slice(0, groups * n_q)
    two = 2 * PAGE_SIZE
    s = jnp.concatenate(
        [jnp.dot(qbd, pair(kt_refs, j, 1), preferred_element_type=F32)
         for j in range(0, pages, 2)], axis=1)
    if fox:
        s = bias(s, ck_ref[0])
    alpha, p = softmax_step(s)
    if fox:
        accumulate(alpha, [(all_rows, [(p[:, j * PAGE_SIZE:j * PAGE_SIZE + two], pair(v_refs, j, 1), NT_DIMS)
                                       for j in range(0, pages, 2)])])
    else:
        accumulate(alpha, [
            (slice(hd * head_rows, (hd + 1) * head_rows),
             [(p[hd * head_rows:(hd + 1) * head_rows, j * PAGE_SIZE:j * PAGE_SIZE + two],
               pair(v_refs, j, 0, (pl.ds(hd, PAGE_SIZE, stride=H_DIFF), slice(None))), nn)
              for j in range(0, pages, 2)])
            for hd in range(H_DIFF)])

    @pl.when(c == pl.num_programs(1) - 1)
    def _():
        pad = jnp.zeros((PAGE_SIZE - n_q, width), BF16)
        kn = jnp.concatenate([knew_ref[0], pad], axis=0)
        vn = jnp.concatenate([vnew_ref[0], pad], axis=0)
        sn = lax.dot_general(qbd, kn, NT_DIMS, preferred_element_type=F32)
        if fox:
            sn = bias(sn, cknew_ref[0])
        r = lax.broadcasted_iota(jnp.int32, sn.shape, 0)
        col = lax.broadcasted_iota(jnp.int32, sn.shape, 1)
        sn = jnp.where(col <= r % n_q, sn, NEG_INF)
        alpha_n, pn = softmax_step(sn)
        if fox:
            accumulate(alpha_n, [(all_rows, [(pn, vn, nn)])])
        else:
            accumulate(alpha_n, [
                (slice(hd * head_rows, (hd + 1) * head_rows),
                 [(pn[hd * head_rows:(hd + 1) * head_rows], vn[:, hd * head_w:(hd + 1) * head_w], nn)])
                for hd in range(H_DIFF)])

        o_full = acc_ref[...] / l_ref[...]
        if fox:
            lane = lax.broadcasted_iota(jnp.int32, (n_q, width), 1)
            o = jnp.zeros((n_q, width), F32)
            for g in range(groups):
                o = o + jnp.where(lane // HD_FOX == g, o_full[g * n_q:(g + 1) * n_q], 0.0)
            o_ref[0] = o.astype(o_ref.dtype)
        else:
            lam = _lambda(lam_ref)
            heads = []
            for hd in range(H_DIFF):
                o0 = o_full[hd * head_rows:hd * head_rows + n_q]
                o1 = o_full[hd * head_rows + n_q:(hd + 1) * head_rows]
                heads.append(_rms(o0 - lam * o1, gs_ref[...]))
            o_ref[0] = (jnp.concatenate(heads, axis=1) * (1.0 - LAM_INIT)).astype(o_ref.dtype)


def _paged_attn(page_table, q, cache_kt, cache_v, k_new, v_new, extra, *, fox, pages):
    b, n_q, width = q.shape
    n_pages = page_table.shape[1]
    steps = n_pages // pages
    page_spec = lambda j: pl.BlockSpec(
        (1,) + cache_kt.shape[1:], lambda b_, c, pt: (pt[b_, c * pages + j], 0, 0))
    per_b = pl.BlockSpec((1, n_q, width), lambda b_, c, pt: (b_, 0, 0))
    in_specs = [per_b] + [page_spec(j) for j in range(pages)] * 2 + [per_b, per_b]
    if fox:
        in_specs += [pl.BlockSpec((1, H_FOX, pages * PAGE_SIZE), lambda b_, c, pt: (b_, 0, c)),
                     pl.BlockSpec((1, H_FOX, PAGE_SIZE), lambda b_, c, pt: (b_, 0, 0))]
    else:
        in_specs += [pl.BlockSpec(e.shape, lambda b_, c, pt: (0, 0)) for e in extra]
    rows = n_q * (width // HD_FOX)
    acc_w = width if fox else 2 * HD_DIFF
    return pl.pallas_call(
        functools.partial(_paged_attn_kernel, fox=fox, pages=pages),
        grid_spec=pltpu.PrefetchScalarGridSpec(
            num_scalar_prefetch=1,
            grid=(b, steps),
            in_specs=in_specs,
            out_specs=per_b,
            scratch_shapes=[pltpu.VMEM((rows, 1), F32), pltpu.VMEM((rows, 1), F32),
                            pltpu.VMEM((rows, acc_w), F32)]),
        out_shape=jax.ShapeDtypeStruct(q.shape, BF16),
        compiler_params=_params(("arbitrary", "arbitrary")),
        name="paged_attn_fox" if fox else "paged_attn_diff",
    )(page_table, q, *([cache_kt] * pages), *([cache_v] * pages), k_new, v_new, *extra)


def _mem_kv_kernel(m_ref, g_ref, wk_ref, wv_ref, k_ref, v_ref, kb_ref, vb_ref):
    mn = _rms(m_ref[...], g_ref[...]).astype(BF16)
    k = jnp.dot(mn, wk_ref[...], preferred_element_type=F32)
    v = jnp.dot(mn, wv_ref[...], preferred_element_type=F32)
    k_ref[...] = k
    v_ref[...] = v
    kb_ref[...] = k.astype(BF16)
    vb_ref[...] = v.astype(BF16)


def _mem_kv(mem, g, wk, wv, *, tm):
    t = mem.shape[0]
    blk = pl.BlockSpec((tm, D_MODEL), lambda i: (i, 0))
    return pl.pallas_call(
        _mem_kv_kernel,
        grid=(t // tm,),
        in_specs=[blk, _resident(g.shape), _resident(wk.shape), _resident(wv.shape)],
        out_specs=[blk] * 4,
        out_shape=[jax.ShapeDtypeStruct(mem.shape, F32)] * 2 + [jax.ShapeDtypeStruct(mem.shape, BF16)] * 2,
        compiler_params=_params(("arbitrary",)),
        name="mem_kv",
    )(mem, g, wk, wv)


def _merge_kernel(x_ref, of_ref, od_ref, ga_ref, gb_ref, wbf_ref, wbd_ref, wmo_ref,
                  gc_ref, wcq_ref, x1_ref, cq_ref):
    merged = (ga_ref[...].astype(F32) * jnp.dot(of_ref[...], wbf_ref[...], preferred_element_type=F32)
              + gb_ref[...].astype(F32) * jnp.dot(od_ref[...], wbd_ref[...], preferred_element_type=F32))
    x1 = x_ref[...] + jnp.dot(merged.astype(BF16), wmo_ref[...], preferred_element_type=F32)
    x1_ref[...] = x1
    hc = _rms(x1, gc_ref[...]).astype(BF16)
    cq = jnp.dot(hc, wcq_ref[...], preferred_element_type=F32) * HD_MEM ** -0.5
    cq_ref[...] = cq.astype(cq_ref.dtype)


def _merge(x, of, od, ga, gb, wbf, wbd, wmo, gc, wcq, *, tm, cq_dtype):
    t = x.shape[0]
    row = lambda i: (i, 0)
    blk = lambda n: pl.BlockSpec((tm, n), row)
    return pl.pallas_call(
        _merge_kernel,
        grid=(t // tm,),
        in_specs=[blk(D_MODEL), blk(W_FOX), blk(W_DIFF), blk(D_MODEL), blk(D_MODEL),
                  _resident(wbf.shape), _resident(wbd.shape), _resident(wmo.shape),
                  _resident(gc.shape), _resident(wcq.shape)],
        out_specs=[blk(D_MODEL), blk(D_MODEL)],
        out_shape=[jax.ShapeDtypeStruct(x.shape, F32), jax.ShapeDtypeStruct(x.shape, cq_dtype)],
        compiler_params=_params(("arbitrary",)),
        name="merge",
    )(x, of, od, ga, gb, wbf, wbd, wmo, gc, wcq)


def _cross_attn_kernel(q_ref, k_ref, v_ref, o_ref):
    for hd in range(H_MEM):
        sl = slice(hd * HD_MEM, (hd + 1) * HD_MEM)
        k = k_ref[0, :, sl]
        v = v_ref[0, :, sl]
        q = q_ref[:, sl].astype(k.dtype)
        s = lax.dot_general(q, k, NT_DIMS, preferred_element_type=F32)
        p = jnp.exp(s - jnp.max(s, axis=1, keepdims=True))
        p = p / jnp.sum(p, axis=1, keepdims=True)
        o_ref[:, sl] = jnp.dot(p.astype(v.dtype), v, preferred_element_type=F32).astype(o_ref.dtype)


def _cross_attn(q, mem_k, mem_v, *, tq, out_dtype):
    t = q.shape[0]
    b, m, _ = mem_k.shape
    tiles = t // b // tq
    qmap = lambda b_, i: (b_ * tiles + i, 0)
    mmap = lambda b_, i: (b_, 0, 0)
    return pl.pallas_call(
        _cross_attn_kernel,
        grid=(b, tiles),
        in_specs=[pl.BlockSpec((tq, D_MODEL), qmap), pl.BlockSpec((1, m, D_MODEL), mmap),
                  pl.BlockSpec((1, m, D_MODEL), mmap)],
        out_specs=pl.BlockSpec((tq, D_MODEL), qmap),
        out_shape=jax.ShapeDtypeStruct(q.shape, out_dtype),
        compiler_params=_params(("arbitrary", "arbitrary")),
        name="cross_attn",
    )(q, mem_k, mem_v)


FF_CHUNK = D_FF // 2


def _ffn_kernel(x1_ref, co_ref, wco_ref, gf_ref, wup_ref, cw_ref, cb_ref, wdn_ref, gfin_ref,
                *rest, group):
    if group is None:
        y_ref, cn_ref, carry_ref = rest
    else:
        e1_ref, e2_ref, y_ref, cn_ref = rest
    tm = x1_ref.shape[0]
    x2 = x1_ref[...] + jnp.dot(co_ref[...].astype(BF16), wco_ref[...], preferred_element_type=F32)
    hf = _rms(x2, gf_ref[...]).astype(BF16)
    pos = lax.broadcasted_iota(jnp.int32, (tm, 1), 0)
    if group is None:
        @pl.when(pl.program_id(1) == 0)
        def _():
            carry_ref[...] = jnp.zeros_like(carry_ref)
    else:
        pos = pos % group

    acc = jnp.zeros((tm, D_MODEL), F32)
    for c0 in range(0, D_FF, FF_CHUNK):
        cs = slice(c0, c0 + FF_CHUNK)
        a = jnp.dot(hf, wup_ref[:, cs], preferred_element_type=F32)
        b = jnp.dot(hf, wup_ref[:, D_FF + c0:D_FF + c0 + FF_CHUNK], preferred_element_type=F32)
        if group is None:
            h0 = carry_ref[0:1, cs]
            h1 = carry_ref[1:2, cs]
            a_m1 = jnp.where(pos == 0, h1, pltpu.roll(a, 1, 0))
            a_m2 = jnp.where(pos == 0, h0, jnp.where(pos == 1, h1, pltpu.roll(a, 2, 0)))
            tail = a[tm - (CONV_W - 1):, :]
            carry_ref[0:CONV_W - 1, cs] = tail
            cn_ref[0, :, cs] = tail
        else:
            a_m1 = jnp.where(pos == 0, e1_ref[:, cs], pltpu.roll(a, 1, 0))
            a_m2 = jnp.where(pos < 2, e2_ref[:, cs], pltpu.roll(a, 2, 0))
            cn_ref[:, :, cs] = a.reshape(tm // group, group, FF_CHUNK)[:, group - (CONV_W - 1):, :]
        ac = cw_ref[0:1, cs] * a_m2 + cw_ref[1:2, cs] * a_m1 + cw_ref[2:3, cs] * a + cb_ref[:, cs]
        gate = ac * jax.nn.sigmoid(ac) * b
        acc = acc + jnp.dot(gate.astype(BF16), wdn_ref[cs, :], preferred_element_type=F32)
    y_ref[...] = _rms(x2 + acc, gfin_ref[...])


def _ffn(x1, co, wco, gf, wup, cw, cb, wdn, gfin, hist, *, tm, batch, group):
    t = x1.shape[0]
    tiles = t // batch // tm if group is None else t // tm
    if group is None:
        grid = (batch, tiles)
        row = lambda b_, i: (b_ * tiles + i, 0)
        cn_spec = pl.BlockSpec((1, CONV_W - 1, D_FF), lambda b_, i: (b_, 0, 0))
        hist_specs = []
        scratch = [pltpu.VMEM((8, D_FF), F32)]
    else:
        grid = (1, tiles)
        row = lambda b_, i: (i, 0)
        cn_spec = pl.BlockSpec((tm // group, CONV_W - 1, D_FF), lambda b_, i: (i, 0, 0))
        hist_specs = [pl.BlockSpec((tm, D_FF), row)] * 2
        scratch = []
    blk = pl.BlockSpec((tm, D_MODEL), row)
    return pl.pallas_call(
        functools.partial(_ffn_kernel, group=group),
        grid=grid,
        in_specs=[blk, blk, _resident(wco.shape), _resident(gf.shape), _resident(wup.shape),
                  _resident(cw.shape), _resident(cb.shape), _resident(wdn.shape),
                  _resident(gfin.shape)] + hist_specs,
        out_specs=[blk, cn_spec],
        out_shape=[jax.ShapeDtypeStruct(x1.shape, F32),
                   jax.ShapeDtypeStruct((batch, CONV_W - 1, D_FF), F32)],
        scratch_shapes=scratch,
        compiler_params=_params(("arbitrary", "arbitrary")),
        name="ffn_seq" if group is None else "ffn_grouped",
    )(x1, co, wco, gf, wup, cw, cb, wdn, gfin, *hist)


def _rope_tables(pos):
    half = HD_DIFF // 2
    inv = ROPE_THETA ** (-jnp.arange(half, dtype=F32) * (2.0 / HD_DIFF))
    ang = pos.astype(F32)[:, None] * inv[None, :]
    cos = jnp.cos(ang)
    sin = jnp.sin(ang)
    return (jnp.concatenate([cos] * 4, axis=1), jnp.concatenate([-sin, sin] * 2, axis=1))


def kernel(x_prompt, x_sample, cache_fox_k, cache_fox_v, cache_fox_logf, cache_diff_k, cache_diff_v, cache_mem_k, cache_mem_v, state_conv, page_table, mem_prompt, g_mix, w_in, b_fgate, lambda_q1, lambda_k1, lambda_q2, lambda_k2, g_subln, w_br_fox, w_br_diff, w_mix_out, g_cross, g_mem, w_cq, w_ck, w_cv, w_co, g_ffn, w_up, conv_w, conv_b, w_down, g_final):
    bp, seq, _ = x_prompt.shape
    bs, dec, _ = x_sample.shape
    n_pool = cache_fox_k.shape[1]
    n_pages = page_table.shape[1]
    past = n_pages * PAGE_SIZE
    l = 0

    w = w_in[l]
    c_fl = 3 * W_FOX
    w_cat = jnp.concatenate([w[:, :c_fl], w[:, c_fl + H_FOX:]], axis=1).astype(BF16)
    w_fl = jnp.pad(w[:, c_fl:c_fl + H_FOX], ((0, 0), (0, LANES - H_FOX))).astype(BF16)
    b_fl = jnp.pad(b_fgate[l], (0, LANES - H_FOX))[None, :]
    row = lambda v: v[None, :]
    lam = jnp.stack([lambda_q1[l], lambda_k1[l], lambda_q2[l], lambda_k2[l]])
    gs = row(g_subln[l])
    wbf, wbd, wmo = w_br_fox[l].astype(BF16), w_br_diff[l].astype(BF16), w_mix_out[l].astype(BF16)
    wcq, wck, wcv, wco = (w_cq[l].astype(BF16), w_ck[l].astype(BF16), w_cv[l].astype(BF16),
                          w_co[l].astype(BF16))
    wup, wdn = w_up[l].astype(BF16), w_down[l].astype(BF16)

    def tail(x1, co, hist, *, tm, batch, group):
        return _ffn(x1, co, wco, row(g_ffn[l]), wup, conv_w[l], row(conv_b[l]), wdn, row(g_final),
                    hist, tm=tm, batch=batch, group=group)

    tp = bp * seq
    xp = x_prompt.reshape(tp, D_MODEL)
    cos_p, sin_p = _rope_tables(jnp.arange(seq, dtype=jnp.int32))
    tm_p = 512
    (fq, fk, fv, fkb, fvb, lf, dq, dk, dv, dkb, dvb, ga, gb, ckt) = _in_proj(
        xp, row(g_mix[l]), w_cat, w_fl, b_fl, cos_p, sin_p, tm=tm_p, tiles_per_seq=seq // tm_p)
    o_fox = _prompt_attn(fq, fkb, fvb, (ckt,), fox=True, batch=bp, seq=seq, tq=256, tk=1024)
    o_diff = _prompt_attn(dq, dkb, dvb, (lam, gs), fox=False, batch=bp, seq=seq, tq=256, tk=1024)

    mem_len = mem_prompt.shape[1]
    mk, mv, mkb, mvb = _mem_kv(mem_prompt.reshape(bp * mem_len, D_MODEL), row(g_mem[l]), wck, wcv, tm=256)
    x1, cq = _merge(xp, o_fox, o_diff, ga, gb, wbf, wbd, wmo, row(g_cross[l]), wcq, tm=512, cq_dtype=BF16)
    co = _cross_attn(cq, mkb.reshape(bp, mem_len, D_MODEL), mvb.reshape(bp, mem_len, D_MODEL), tq=512,
                     out_dtype=BF16)
    y_p, p_conv = tail(x1, co, (), tm=256, batch=bp, group=None)

    ts = bs * dec
    xs = x_sample.reshape(ts, D_MODEL)
    tm_s = 256
    pos_s = past + (jnp.arange(tm_s, dtype=jnp.int32) % dec)
    cos_s, sin_s = _rope_tables(pos_s)
    (sfq, sfk, sfv, sfkb, sfvb, slf, sdq, sdk, sdv, sdkb, sdvb, sga, sgb) = _in_proj(
        xs, row(g_mix[l]), w_cat, w_fl, b_fl, cos_s, sin_s, tm=tm_s, tiles_per_seq=None)

    def tokens_minor(c):
        c = c[l]
        return jnp.moveaxis(c, 1, -1).reshape(n_pool, -1, PAGE_SIZE)

    lf_new_page = jnp.pad(jnp.swapaxes(slf.reshape(bs, dec, H_FOX), 1, 2),
                          ((0, 0), (0, 0), (0, PAGE_SIZE - dec)))
    ck_past, ck_new = _page_cum(page_table, tokens_minor(cache_fox_logf), lf_new_page, pages=16)
    b3 = lambda a: a.reshape(bs, dec, a.shape[-1])
    so_fox = _paged_attn(page_table, b3(sfq), tokens_minor(cache_fox_k), tokens_minor(cache_fox_v),
                         b3(sfkb), b3(sfvb), (ck_past, ck_new), fox=True, pages=16)
    diff_v = cache_diff_v[l].reshape(n_pool, PAGE_SIZE * H_DIFF, 2 * HD_DIFF)
    so_diff = _paged_attn(page_table, b3(sdq), tokens_minor(cache_diff_k), diff_v,
                          b3(sdkb), b3(sdvb), (lam, gs), fox=False, pages=16)

    sx1, scq = _merge(xs, so_fox.reshape(ts, W_FOX), so_diff.reshape(ts, W_DIFF), sga, sgb, wbf, wbd, wmo,
                      row(g_cross[l]), wcq, tm=tm_s, cq_dtype=F32)
    smem = lambda c: c[l].reshape(bs, c.shape[2], D_MODEL)
    sco = _cross_attn(scq, smem(cache_mem_k), smem(cache_mem_v), tq=dec, out_dtype=F32)
    st = state_conv[l]
    e2 = jnp.pad(st, ((0, 0), (0, dec - 2), (0, 0))).reshape(ts, D_FF)
    e1 = jnp.pad(st[:, 1:], ((0, 0), (0, dec - 1), (0, 0))).reshape(ts, D_FF)
    y_s, s_conv = tail(sx1, sco, (e1, e2), tm=tm_s, batch=bs, group=dec)

    d1 = lambda a, *shape: a.reshape((1,) + shape)
    return (y_p.reshape(bp, seq, D_MODEL), y_s.reshape(bs, dec, D_MODEL),
            d1(fk, bp, seq, H_FOX, HD_FOX), d1(fv, bp, seq, H_FOX, HD_FOX), d1(lf, bp, seq, H_FOX),
            d1(dk, bp, seq, H_DIFF, 2, HD_DIFF), d1(dv, bp, seq, H_DIFF, 2 * HD_DIFF),
            d1(mk, bp, mem_len, H_MEM, HD_MEM), d1(mv, bp, mem_len, H_MEM, HD_MEM),
            d1(p_conv, bp, CONV_W - 1, D_FF),
            d1(sfk, bs, dec, H_FOX, HD_FOX), d1(sfv, bs, dec, H_FOX, HD_FOX), d1(slf, bs, dec, H_FOX),
            d1(sdk, bs, dec, H_DIFF, 2, HD_DIFF), d1(sdv, bs, dec, H_DIFF, 2 * HD_DIFF),
            d1(s_conv, bs, CONV_W - 1, D_FF))
```

```python
import functools
import math

import jax
import jax.numpy as jnp
from jax import lax
from jax.experimental import pallas as pl
from jax.experimental.pallas import tpu as pltpu

F32 = jnp.float32
BF16 = jnp.bfloat16

D_MODEL = 1024
H_FOX = 8
HD_FOX = 64
W_FOX = H_FOX * HD_FOX
H_DIFF = 4
HD_DIFF = 64
W_DIFF = H_DIFF * 2 * HD_DIFF
H_MEM = 4
HD_MEM = D_MODEL // H_MEM
D_FF = 2816
CONV_W = 3
PAGE_SIZE = 128
ROPE_THETA = 10000.0
EPS = 1e-6
LAM_INIT = 0.8 - 0.6 * math.exp(-0.3 * 0)
LOG2E = math.log2(math.e)

LANES = 128
V7X_VMEM_BYTES = 64 * 1024 * 1024
VMEM_LIMIT = V7X_VMEM_BYTES - 8 * 1024 * 1024
NEG_INF = float("-inf")

NT_DIMS = (((1,), (1,)), ((), ()))
TN_DIMS = (((0,), (0,)), ((), ()))


def _rms(x, g):
    r = lax.rsqrt(jnp.mean(x * x, axis=-1, keepdims=True) + EPS)
    return x * r * g


def _resident(shape):
    nd = len(shape)
    return pl.BlockSpec(shape, lambda *_: (0,) * nd, pipeline_mode=pl.Buffered(1))


def _params(semantics):
    return pltpu.CompilerParams(dimension_semantics=semantics, vmem_limit_bytes=VMEM_LIMIT)


def _rope(x, cos, sin_signed):
    n = x.shape[1]
    lane = lax.broadcasted_iota(jnp.int32, x.shape, 1)
    partner = jnp.where(lane % HD_DIFF < HD_DIFF // 2,
                        pltpu.roll(x, n - HD_DIFF // 2, 1),
                        pltpu.roll(x, HD_DIFF // 2, 1))
    reps = n // LANES
    c = jnp.concatenate([cos] * reps, axis=1)
    s = jnp.concatenate([sin_signed] * reps, axis=1)
    return x * c + partner * s


def _in_proj_kernel(x_ref, g_ref, w_ref, wfl_ref, bfl_ref, cos_ref, sin_ref,
                    fq_ref, fk_ref, fv_ref, fkb_ref, fvb_ref, lf_ref,
                    dq_ref, dk_ref, dv_ref, dkb_ref, dvb_ref, ga_ref, gb_ref,
                    *rest, tiles_per_seq):
    h = _rms(x_ref[...], g_ref[...]).astype(BF16)

    def proj(c0, n):
        return jnp.dot(h, w_ref[:, c0:c0 + n], preferred_element_type=F32)

    def store_kv(ref, val):
        if tiles_per_seq is None:
            ref[...] = val
        else:
            ref[0] = val.T

    fq_ref[...] = (proj(0, W_FOX) * (HD_FOX ** -0.5 * LOG2E)).astype(BF16)
    fk = proj(W_FOX, W_FOX)
    store_kv(fk_ref, fk)
    fkb_ref[...] = fk.astype(BF16)
    fv = proj(2 * W_FOX, W_FOX)
    store_kv(fv_ref, fv)
    fvb_ref[...] = fv.astype(BF16)

    u = jnp.dot(h, wfl_ref[...], preferred_element_type=F32) + bfl_ref[...]
    lf = jnp.minimum(u, 0.0) - jnp.log1p(jnp.exp(-jnp.abs(u)))
    lf_ref[...] = lf[:, :H_FOX]

    cos = cos_ref[...]
    sin = sin_ref[...]
    c0 = 3 * W_FOX
    dq_ref[...] = (_rope(proj(c0, W_DIFF), cos, sin) * (HD_DIFF ** -0.5 * LOG2E)).astype(BF16)
    dk = _rope(proj(c0 + W_DIFF, W_DIFF), cos, sin)
    store_kv(dk_ref, dk)
    dkb_ref[...] = dk.astype(BF16)
    dv = proj(c0 + 2 * W_DIFF, W_DIFF)
    dv_ref[...] = dv
    dvb_ref[...] = dv.astype(BF16)
    c0 += 3 * W_DIFF
    ga_ref[...] = jax.nn.sigmoid(proj(c0, D_MODEL)).astype(BF16)
    gb_ref[...] = jax.nn.sigmoid(proj(c0 + D_MODEL, D_MODEL)).astype(BF16)

    if tiles_per_seq is not None:
        ck_ref, carry_ref = rest
        tm = lf.shape[0]

        @pl.when(pl.program_id(0) % tiles_per_seq == 0)
        def _():
            carry_ref[...] = jnp.zeros_like(carry_ref)

        row = lax.broadcasted_iota(jnp.int32, (tm, tm), 0)
        col = lax.broadcasted_iota(jnp.int32, (tm, tm), 1)
        tri = (row >= col).astype(F32)
        cum = jnp.dot(tri, lf, precision=lax.Precision.HIGHEST,
                      preferred_element_type=F32) + carry_ref[...]
        carry_ref[...] = cum[tm - 1:tm, :]
        ck_ref[...] = cum.T[:H_FOX, :]


def _in_proj(x, g, w_cat, w_fl, b_fl, cos, sin, *, tm, tiles_per_seq):
    t = x.shape[0]
    n_tiles = t // tm
    with_cum = tiles_per_seq is not None
    table_tiles = cos.shape[0] // tm
    row = lambda i: (i, 0)
    tab = lambda i: (i % table_tiles, 0)
    wide = lambda n, dt: jax.ShapeDtypeStruct((t, n), dt)
    blk = lambda n: pl.BlockSpec((tm, n), row)
    if with_cum:
        kv = lambda n: jax.ShapeDtypeStruct((n_tiles // tiles_per_seq, n, tiles_per_seq * tm), F32)
        kv_blk = lambda n: pl.BlockSpec((1, n, tm), lambda i: (i // tiles_per_seq, 0, i % tiles_per_seq))
    else:
        kv = lambda n: wide(n, F32)
        kv_blk = blk
    out_shape = [wide(W_FOX, BF16), kv(W_FOX), kv(W_FOX), wide(W_FOX, BF16),
                 wide(W_FOX, BF16), wide(H_FOX, F32),
                 wide(W_DIFF, BF16), kv(W_DIFF), wide(W_DIFF, F32), wide(W_DIFF, BF16),
                 wide(W_DIFF, BF16), wide(D_MODEL, BF16), wide(D_MODEL, BF16)]
    out_specs = ([blk(W_FOX), kv_blk(W_FOX), kv_blk(W_FOX), blk(W_FOX), blk(W_FOX), blk(H_FOX),
                  blk(W_DIFF), kv_blk(W_DIFF)] + [blk(W_DIFF)] * 3 + [blk(D_MODEL)] * 2)
    scratch = []
    if with_cum:
        out_shape.append(jax.ShapeDtypeStruct((H_FOX, t), F32))
        out_specs.append(pl.BlockSpec((H_FOX, tm), lambda i: (0, i)))
        scratch.append(pltpu.VMEM((1, LANES), F32))
    return pl.pallas_call(
        functools.partial(_in_proj_kernel, tiles_per_seq=tiles_per_seq),
        grid=(n_tiles,),
        in_specs=[blk(D_MODEL), _resident(g.shape), _resident(w_cat.shape),
                  _resident(w_fl.shape), _resident(b_fl.shape),
                  pl.BlockSpec((tm, LANES), tab), pl.BlockSpec((tm, LANES), tab)],
        out_specs=out_specs,
        out_shape=out_shape,
        scratch_shapes=scratch,
        compiler_params=_params(("arbitrary",)),
        name="in_proj_cum" if with_cum else "in_proj",
    )(x, g, w_cat, w_fl, b_fl, cos, sin)


def _lambda(lam_ref):
    lp = lam_ref[...]
    d1 = jnp.sum(lp[0:1] * lp[1:2], axis=1, keepdims=True)
    d2 = jnp.sum(lp[2:3] * lp[3:4], axis=1, keepdims=True)
    return jnp.exp(d1) - jnp.exp(d2) + LAM_INIT


def _prompt_attn_kernel(qt_ref, kt_ref, diag_ref, q_ref, k_ref, v_ref, *rest, fox):
    if fox:
        ck_ref, o_ref, m_ref, l_ref, acc_ref = rest
    else:
        lam_ref, gs_ref, o_ref, m_ref, l_ref, acc_ref = rest
    t = pl.program_id(1)
    ki = kt_ref[t]
    tq = q_ref.shape[0]
    tk = k_ref.shape[0]
    pairs = q_ref.shape[1] // LANES
    low = lax.broadcasted_iota(jnp.int32, (tq, LANES), 1) < HD_FOX

    @pl.when(ki == 0)
    def _():
        m_ref[...] = jnp.full_like(m_ref, NEG_INF)
        l_ref[...] = jnp.zeros_like(l_ref)
        acc_ref[...] = jnp.zeros_like(acc_ref)

    def step(ncols, masked):
        if masked:
            row = lax.broadcasted_iota(jnp.int32, (tq, tq), 0)
            col = lax.broadcasted_iota(jnp.int32, (tq, tq), 1)
            visible = col <= row
        logits = []
        for pr in range(pairs):
            cols = slice(pr * LANES, (pr + 1) * LANES)
            q = q_ref[:, cols]
            k = k_ref[0:ncols, cols]
            for half in range(2):
                qh = jnp.where(low if half == 0 else jnp.logical_not(low), q, jnp.zeros_like(q))
                logits.append(lax.dot_general(qh, k, NT_DIMS, preferred_element_type=F32))
        probs = []
        for idx, s in enumerate(logits):
            if fox:
                s = s - ck_ref[idx:idx + 1, 0:ncols] * LOG2E
            if masked:
                edge = jnp.where(visible, s[:, ncols - tq:], NEG_INF)
                s = edge if ncols == tq else jnp.concatenate([s[:, :ncols - tq], edge], axis=1)
            m_prev = m_ref[idx]
            m_new = jnp.maximum(m_prev, jnp.max(s, axis=1, keepdims=True))
            alpha = jnp.exp2(m_prev - m_new)
            m_ref[idx] = m_new
            probs.append((alpha, jnp.exp2(s - m_new).astype(BF16)))
        ones = jnp.ones((ncols, LANES), BF16)
        for idx, (alpha, p) in enumerate(probs):
            pr = idx // 2
            v = jnp.concatenate([v_ref[0:ncols, pr * LANES:(pr + 1) * LANES], ones], axis=1)
            pv = jnp.dot(p, v, preferred_element_type=F32)
            acc_ref[idx] = alpha * acc_ref[idx] + pv[:, :LANES]
            l_ref[idx] = alpha * l_ref[idx] + pv[:, LANES:]

    def finalize():
        lam = None if fox else _lambda(lam_ref)
        for pr in range(pairs):
            cols = slice(pr * LANES, (pr + 1) * LANES)
            o0 = acc_ref[2 * pr] / l_ref[2 * pr]
            o1 = acc_ref[2 * pr + 1] / l_ref[2 * pr + 1]
            if fox:
                o_ref[:, cols] = jnp.where(low, o0, o1).astype(o_ref.dtype)
            else:
                o = o0 - lam * o1
                o_ref[:, cols] = (_rms(o, gs_ref[...]) * (1.0 - LAM_INIT)).astype(o_ref.dtype)

    @pl.when(diag_ref[t] == 0)
    def _():
        step(tk, False)

    for j in range(1, tk // tq + 1):
        @pl.when(diag_ref[t] == j)
        def _(j=j):
            step(j * tq, True)
            finalize()


def _prompt_attn(q, k, v, extra, *, fox, batch, seq, tq, tk):
    nq = seq // tq
    nk = seq // tk
    width = q.shape[1]
    maps = width // HD_FOX
    sched = [(i, j) for i in range(nq) for j in range((i * tq) // tk + 1)]
    qt = jnp.asarray([i for i, _ in sched], jnp.int32)
    kt = jnp.asarray([j for _, j in sched], jnp.int32)
    diag = jnp.asarray([((i * tq) % tk) // tq + 1 if j == (i * tq) // tk else 0 for i, j in sched],
                       jnp.int32)
    qmap = lambda b, t, qt, kt, diag: (b * nq + qt[t], 0)
    kmap = lambda b, t, qt, kt, diag: (b * nk + kt[t], 0)
    in_specs = [pl.BlockSpec((tq, width), qmap), pl.BlockSpec((tk, width), kmap),
                pl.BlockSpec((tk, width), kmap)]
    if fox:
        in_specs.append(pl.BlockSpec((maps, tk), lambda b, t, qt, kt, diag: (0, b * nk + kt[t])))
    else:
        in_specs += [pl.BlockSpec(e.shape, lambda b, t, qt, kt, diag: (0, 0)) for e in extra]
    return pl.pallas_call(
        functools.partial(_prompt_attn_kernel, fox=fox),
        grid_spec=pltpu.PrefetchScalarGridSpec(
            num_scalar_prefetch=3,
            grid=(batch, len(sched)),
            in_specs=in_specs,
            out_specs=pl.BlockSpec((tq, width), qmap),
            scratch_shapes=[pltpu.VMEM((maps, tq, 1), F32), pltpu.VMEM((maps, tq, LANES), F32),
                            pltpu.VMEM((maps, tq, LANES), F32)]),
        out_shape=jax.ShapeDtypeStruct(q.shape, BF16),
        compiler_params=_params(("arbitrary", "arbitrary")),
        name="prompt_attn_fox" if fox else "prompt_attn_diff",
    )(qt, kt, diag, q, k, v, *extra)


def _page_cum_kernel(pt_ref, *refs, pages):
    lf_refs = refs[:pages]
    new_ref, ck_ref, cknew_ref, carry_ref = refs[pages:]
    c = pl.program_id(1)

    @pl.when(c == 0)
    def _():
        carry_ref[...] = jnp.zeros_like(carry_ref)

    row = lax.broadcasted_iota(jnp.int32, (PAGE_SIZE, PAGE_SIZE), 0)
    col = lax.broadcasted_iota(jnp.int32, (PAGE_SIZE, PAGE_SIZE), 1)
    tri = (row <= col).astype(F32)

    def page_cum(x):
        return jnp.dot(x, tri, precision=lax.Precision.HIGHEST, preferred_element_type=F32)

    within = page_cum(jnp.concatenate([r[0] for r in lf_refs], axis=0))
    carry = carry_ref[...]
    for j in range(pages):
        cj = within[j * H_FOX:(j + 1) * H_FOX]
        ck_ref[0, :, j * PAGE_SIZE:(j + 1) * PAGE_SIZE] = cj + carry
        carry = carry + cj[:, PAGE_SIZE - 1:PAGE_SIZE]
    carry_ref[...] = carry

    @pl.when(c == pl.num_programs(1) - 1)
    def _():
        cknew_ref[0] = page_cum(new_ref[0]) + carry


def _page_cum(page_table, cache_lf, lf_new_page, *, pages):
    b, n_pages = page_table.shape
    steps = n_pages // pages
    in_specs = [pl.BlockSpec((1, H_FOX, PAGE_SIZE),
                             functools.partial(lambda b_, c, pt, j: (pt[b_, c * pages + j], 0, 0), j=j))
                for j in range(pages)]
    in_specs.append(pl.BlockSpec((1, H_FOX, PAGE_SIZE), lambda b_, c, pt: (b_, 0, 0)))
    return pl.pallas_call(
        functools.partial(_page_cum_kernel, pages=pages),
        grid_spec=pltpu.PrefetchScalarGridSpec(
            num_scalar_prefetch=1,
            grid=(b, steps),
            in_specs=in_specs,
            out_specs=[pl.BlockSpec((1, H_FOX, pages * PAGE_SIZE), lambda b_, c, pt: (b_, 0, c)),
                       pl.BlockSpec((1, H_FOX, PAGE_SIZE), lambda b_, c, pt: (b_, 0, 0))],
            scratch_shapes=[pltpu.VMEM((H_FOX, 1), F32)]),
        out_shape=[jax.ShapeDtypeStruct((b, H_FOX, n_pages * PAGE_SIZE), F32),
                   jax.ShapeDtypeStruct((b, H_FOX, PAGE_SIZE), F32)],
        compiler_params=_params(("arbitrary", "arbitrary")),
        name="page_cum",
    )(page_table, *([cache_lf] * pages), lf_new_page)


def _block_diag_q(q):
    n_groups = q.shape[1] // HD_FOX
    rows = q.shape[0]
    qq = jnp.concatenate([q] * n_groups, axis=0)
    r = lax.broadcasted_iota(jnp.int32, qq.shape, 0)
    c = lax.broadcasted_iota(jnp.int32, qq.shape, 1)
    return jnp.where(r // rows == c // HD_FOX, qq, jnp.zeros_like(qq))


def _paged_attn_kernel(pt_ref, *refs, fox, pages):
    q_ref = refs[0]
    kt_refs = refs[1:1 + pages]
    v_refs = refs[1 + pages:1 + 2 * pages]
    rest = refs[1 + 2 * pages:]
    knew_ref, vnew_ref = rest[:2]
    if fox:
        ck_ref, cknew_ref, o_ref, m_ref, l_ref, acc_ref = rest[2:]
    else:
        lam_ref, gs_ref, o_ref, m_ref, l_ref, acc_ref = rest[2:]
    c = pl.program_id(1)
    n_q = q_ref.shape[1]
    width = q_ref.shape[2]
    groups = width // HD_FOX
    head_rows = 2 * n_q
    head_w = 2 * HD_DIFF

    @pl.when(c == 0)
    def _():
        m_ref[...] = jnp.full_like(m_ref, NEG_INF)
        l_ref[...] = jnp.zeros_like(l_ref)
        acc_ref[...] = jnp.zeros_like(acc_ref)

    qbd = _block_diag_q(q_ref[0])

    def bias(s, ck):
        ck = ck * LOG2E
        return jnp.concatenate(
            [s[g * n_q:(g + 1) * n_q] - ck[g:g + 1] for g in range(groups)], axis=0)

    def softmax_step(s):
        m_prev = m_ref[...]
        m_new = jnp.maximum(m_prev, jnp.max(s, axis=1, keepdims=True))
        alpha = jnp.exp2(m_prev - m_new)
        p = jnp.exp2(s - m_new)
        l_ref[...] = alpha * l_ref[...] + jnp.sum(p, axis=1, keepdims=True)
        m_ref[...] = m_new
        return alpha, p.astype(BF16)

    def pair(refs_, j, axis, index=()):
        return jnp.concatenate([refs_[j][(0,) + index].astype(BF16),
                                refs_[j + 1][(0,) + index].astype(BF16)], axis=axis)

    def accumulate(alpha, terms):
        for rows, prods in terms:
            pv = None
            for lhs, rhs, dims in prods:
                d = lax.dot_general(lhs, rhs, dims, preferred_element_type=F32)
                pv = d if pv is None else pv + d
            acc_ref[rows, :] = alpha[rows] * acc_ref[rows, :] + pv

    nn = (((1,), (0,)), ((), ()))
    all_rows = slice(0, groups * n_q)
    two = 2 * PAGE_SIZE
    s = jnp.concatenate(
        [jnp.dot(qbd, pair(kt_refs, j, 1), preferred_element_type=F32)
         for j in range(0, pages, 2)], axis=1)
    if fox:
        s = bias(s, ck_ref[0])
    alpha, p = softmax_step(s)
    if fox:
        accumulate(alpha, [(all_rows, [(p[:, j * PAGE_SIZE:j * PAGE_SIZE + two], pair(v_refs, j, 1), NT_DIMS)
                                       for j in range(0, pages, 2)])])
    else:
        accumulate(alpha, [
            (slice(hd * head_rows, (hd + 1) * head_rows),
             [(p[hd * head_rows:(hd + 1) * head_rows, j * PAGE_SIZE:j * PAGE_SIZE + two],
               pair(v_refs, j, 0, (pl.ds(hd, PAGE_SIZE, stride=H_DIFF), slice(None))), nn)
              for j in range(0, pages, 2)])
            for hd in range(H_DIFF)])

    @pl.when(c == pl.num_programs(1) - 1)
    def _():
        pad = jnp.zeros((PAGE_SIZE - n_q, width), BF16)
        kn = jnp.concatenate([knew_ref[0], pad], axis=0)
        vn = jnp.concatenate([vnew_ref[0], pad], axis=0)
        sn = lax.dot_general(qbd, kn, NT_DIMS, preferred_element_type=F32)
        if fox:
            sn = bias(sn, cknew_ref[0])
        r = lax.broadcasted_iota(jnp.int32, sn.shape, 0)
        col = lax.broadcasted_iota(jnp.int32, sn.shape, 1)
        sn = jnp.where(col <= r % n_q, sn, NEG_INF)
        alpha_n, pn = softmax_step(sn)
        if fox:
            accumulate(alpha_n, [(all_rows, [(pn, vn, nn)])])
        else:
            accumulate(alpha_n, [
                (slice(hd * head_rows, (hd + 1) * head_rows),
                 [(pn[hd * head_rows:(hd + 1) * head_rows], vn[:, hd * head_w:(hd + 1) * head_w], nn)])
                for hd in range(H_DIFF)])

        o_full = acc_ref[...] / l_ref[...]
        if fox:
            lane = lax.broadcasted_iota(jnp.int32, (n_q, width), 1)
            o = jnp.zeros((n_q, width), F32)
            for g in range(groups):
                o = o + jnp.where(lane // HD_FOX == g, o_full[g * n_q:(g + 1) * n_q], 0.0)
            o_ref[0] = o.astype(o_ref.dtype)
        else:
            lam = _lambda(lam_ref)
            heads = []
            for hd in range(H_DIFF):
                o0 = o_full[hd * head_rows:hd * head_rows + n_q]
                o1 = o_full[hd * head_rows + n_q:(hd + 1) * head_rows]
                heads.append(_rms(o0 - lam * o1, gs_ref[...]))
            o_ref[0] = (jnp.concatenate(heads, axis=1) * (1.0 - LAM_INIT)).astype(o_ref.dtype)


def _paged_attn(page_table, q, cache_kt, cache_v, k_new, v_new, extra, *, fox, pages):
    b, n_q, width = q.shape
    n_pages = page_table.shape[1]
    steps = n_pages // pages
    page_spec = lambda j: pl.BlockSpec(
        (1,) + cache_kt.shape[1:], lambda b_, c, pt: (pt[b_, c * pages + j], 0, 0))
    per_b = pl.BlockSpec((1, n_q, width), lambda b_, c, pt: (b_, 0, 0))
    in_specs = [per_b] + [page_spec(j) for j in range(pages)] * 2 + [per_b, per_b]
    if fox:
        in_specs += [pl.BlockSpec((1, H_FOX, pages * PAGE_SIZE), lambda b_, c, pt: (b_, 0, c)),
                     pl.BlockSpec((1, H_FOX, PAGE_SIZE), lambda b_, c, pt: (b_, 0, 0))]
    else:
        in_specs += [pl.BlockSpec(e.shape, lambda b_, c, pt: (0, 0)) for e in extra]
    rows = n_q * (width // HD_FOX)
    acc_w = width if fox else 2 * HD_DIFF
    return pl.pallas_call(
        functools.partial(_paged_attn_kernel, fox=fox, pages=pages),
        grid_spec=pltpu.PrefetchScalarGridSpec(
            num_scalar_prefetch=1,
            grid=(b, steps),
            in_specs=in_specs,
            out_specs=per_b,
            scratch_shapes=[pltpu.VMEM((rows, 1), F32), pltpu.VMEM((rows, 1), F32),
                            pltpu.VMEM((rows, acc_w), F32)]),
        out_shape=jax.ShapeDtypeStruct(q.shape, BF16),
        compiler_params=_params(("arbitrary", "arbitrary")),
        name="paged_attn_fox" if fox else "paged_attn_diff",
    )(page_table, q, *([cache_kt] * pages), *([cache_v] * pages), k_new, v_new, *extra)


def _mem_kv_kernel(m_ref, g_ref, wk_ref, wv_ref, k_ref, v_ref, kb_ref, vb_ref):
    mn = _rms(m_ref[...], g_ref[...]).astype(BF16)
    k = jnp.dot(mn, wk_ref[...], preferred_element_type=F32)
    v = jnp.dot(mn, wv_ref[...], preferred_element_type=F32)
    k_ref[...] = k
    v_ref[...] = v
    kb_ref[...] = k.astype(BF16)
    vb_ref[...] = v.astype(BF16)


def _mem_kv(mem, g, wk, wv, *, tm):
    t = mem.shape[0]
    blk = pl.BlockSpec((tm, D_MODEL), lambda i: (i, 0))
    return pl.pallas_call(
        _mem_kv_kernel,
        grid=(t // tm,),
        in_specs=[blk, _resident(g.shape), _resident(wk.shape), _resident(wv.shape)],
        out_specs=[blk] * 4,
        out_shape=[jax.ShapeDtypeStruct(mem.shape, F32)] * 2 + [jax.ShapeDtypeStruct(mem.shape, BF16)] * 2,
        compiler_params=_params(("arbitrary",)),
        name="mem_kv",
    )(mem, g, wk, wv)


def _merge_kernel(x_ref, of_ref, od_ref, ga_ref, gb_ref, wbf_ref, wbd_ref, wmo_ref,
                  gc_ref, wcq_ref, x1_ref, cq_ref):
    merged = (ga_ref[...].astype(F32) * jnp.dot(of_ref[...], wbf_ref[...], preferred_element_type=F32)
              + gb_ref[...].astype(F32) * jnp.dot(od_ref[...], wbd_ref[...], preferred_element_type=F32))
    x1 = x_ref[...] + jnp.dot(merged.astype(BF16), wmo_ref[...], preferred_element_type=F32)
    x1_ref[...] = x1
    hc = _rms(x1, gc_ref[...]).astype(BF16)
    cq = jnp.dot(hc, wcq_ref[...], preferred_element_type=F32) * HD_MEM ** -0.5
    cq_ref[...] = cq.astype(cq_ref.dtype)


def _merge(x, of, od, ga, gb, wbf, wbd, wmo, gc, wcq, *, tm, cq_dtype):
    t = x.shape[0]
    row = lambda i: (i, 0)
    blk = lambda n: pl.BlockSpec((tm, n), row)
    return pl.pallas_call(
        _merge_kernel,
        grid=(t // tm,),
        in_specs=[blk(D_MODEL), blk(W_FOX), blk(W_DIFF), blk(D_MODEL), blk(D_MODEL),
                  _resident(wbf.shape), _resident(wbd.shape), _resident(wmo.shape),
                  _resident(gc.shape), _resident(wcq.shape)],
        out_specs=[blk(D_MODEL), blk(D_MODEL)],
        out_shape=[jax.ShapeDtypeStruct(x.shape, F32), jax.ShapeDtypeStruct(x.shape, cq_dtype)],
        compiler_params=_params(("arbitrary",)),
        name="merge",
    )(x, of, od, ga, gb, wbf, wbd, wmo, gc, wcq)


def _cross_attn_kernel(q_ref, k_ref, v_ref, o_ref, *, tiled):
    halves = HD_MEM // LANES
    for hd in range(H_MEM):
        sl = slice(hd * HD_MEM, (hd + 1) * HD_MEM)
        if tiled:
            m = k_ref.shape[1] // (halves * H_MEM)
            rows = lambda c: pl.ds(c * H_MEM + hd, m, stride=halves * H_MEM)
            k = jnp.concatenate([k_ref[0, rows(c), :] for c in range(halves)], axis=1)
            v = jnp.concatenate([v_ref[0, rows(c), :] for c in range(halves)], axis=1)
        else:
            k = k_ref[0, :, sl]
            v = v_ref[0, :, sl]
        q = q_ref[:, sl].astype(k.dtype)
        s = lax.dot_general(q, k, NT_DIMS, preferred_element_type=F32)
        p = jnp.exp(s - jnp.max(s, axis=1, keepdims=True))
        p = p / jnp.sum(p, axis=1, keepdims=True)
        o_ref[:, sl] = jnp.dot(p.astype(v.dtype), v, preferred_element_type=F32).astype(o_ref.dtype)


def _cross_attn(q, mem_k, mem_v, *, tq, out_dtype, tiled):
    t = q.shape[0]
    b = mem_k.shape[0]
    tiles = t // b // tq
    qmap = lambda b_, i: (b_ * tiles + i, 0)
    mmap = lambda b_, i: (b_, 0, 0)
    mem_spec = pl.BlockSpec((1,) + mem_k.shape[1:], mmap)
    return pl.pallas_call(
        functools.partial(_cross_attn_kernel, tiled=tiled),
        grid=(b, tiles),
        in_specs=[pl.BlockSpec((tq, D_MODEL), qmap), mem_spec, mem_spec],
        out_specs=pl.BlockSpec((tq, D_MODEL), qmap),
        out_shape=jax.ShapeDtypeStruct(q.shape, out_dtype),
        compiler_params=_params(("arbitrary", "arbitrary")),
        name="cross_attn",
    )(q, mem_k, mem_v)


FF_CHUNK = D_FF // 2


def _ffn_kernel(x1_ref, co_ref, wco_ref, gf_ref, wup_ref, cw_ref, cb_ref, wdn_ref, gfin_ref,
                *rest, group):
    if group is None:
        y_ref, cn_ref, carry_ref = rest
    else:
        e1_ref, e2_ref, y_ref, cn_ref = rest
    tm = x1_ref.shape[0]
    x2 = x1_ref[...] + jnp.dot(co_ref[...].astype(BF16), wco_ref[...], preferred_element_type=F32)
    hf = _rms(x2, gf_ref[...]).astype(BF16)
    pos = lax.broadcasted_iota(jnp.int32, (tm, 1), 0)
    if group is None:
        @pl.when(pl.program_id(1) == 0)
        def _():
            carry_ref[...] = jnp.zeros_like(carry_ref)
    else:
        pos = pos % group

    acc = jnp.zeros((tm, D_MODEL), F32)
    for c0 in range(0, D_FF, FF_CHUNK):
        cs = slice(c0, c0 + FF_CHUNK)
        a = jnp.dot(hf, wup_ref[:, cs], preferred_element_type=F32)
        b = jnp.dot(hf, wup_ref[:, D_FF + c0:D_FF + c0 + FF_CHUNK], preferred_element_type=F32)
        if group is None:
            h0 = carry_ref[0:1, cs]
            h1 = carry_ref[1:2, cs]
            a_m1 = jnp.where(pos == 0, h1, pltpu.roll(a, 1, 0))
            a_m2 = jnp.where(pos == 0, h0, jnp.where(pos == 1, h1, pltpu.roll(a, 2, 0)))
            tail = a[tm - (CONV_W - 1):, :]
            carry_ref[0:CONV_W - 1, cs] = tail
            cn_ref[0, :, cs] = tail
        else:
            a_m1 = jnp.where(pos == 0, e1_ref[:, cs], pltpu.roll(a, 1, 0))
            a_m2 = jnp.where(pos < 2, e2_ref[:, cs], pltpu.roll(a, 2, 0))
            cn_ref[:, :, cs] = a.reshape(tm // group, group, FF_CHUNK)[:, group - (CONV_W - 1):, :]
        ac = cw_ref[0:1, cs] * a_m2 + cw_ref[1:2, cs] * a_m1 + cw_ref[2:3, cs] * a + cb_ref[:, cs]
        gate = ac * jax.nn.sigmoid(ac) * b
        acc = acc + jnp.dot(gate.astype(BF16), wdn_ref[cs, :], preferred_element_type=F32)
    y_ref[...] = _rms(x2 + acc, gfin_ref[...])


def _ffn(x1, co, wco, gf, wup, cw, cb, wdn, gfin, hist, *, tm, batch, group):
    t = x1.shape[0]
    tiles = t // batch // tm if group is None else t // tm
    if group is None:
        grid = (batch, tiles)
        row = lambda b_, i: (b_ * tiles + i, 0)
        cn_spec = pl.BlockSpec((1, CONV_W - 1, D_FF), lambda b_, i: (b_, 0, 0))
        hist_specs = []
        scratch = [pltpu.VMEM((8, D_FF), F32)]
    else:
        grid = (1, tiles)
        row = lambda b_, i: (i, 0)
        cn_spec = pl.BlockSpec((tm // group, CONV_W - 1, D_FF), lambda b_, i: (i, 0, 0))
        hist_specs = [pl.BlockSpec((tm, D_FF), row)] * 2
        scratch = []
    blk = pl.BlockSpec((tm, D_MODEL), row)
    return pl.pallas_call(
        functools.partial(_ffn_kernel, group=group),
        grid=grid,
        in_specs=[blk, blk, _resident(wco.shape), _resident(gf.shape), _resident(wup.shape),
                  _resident(cw.shape), _resident(cb.shape), _resident(wdn.shape),
                  _resident(gfin.shape)] + hist_specs,
        out_specs=[blk, cn_spec],
        out_shape=[jax.ShapeDtypeStruct(x1.shape, F32),
                   jax.ShapeDtypeStruct((batch, CONV_W - 1, D_FF), F32)],
        scratch_shapes=scratch,
        compiler_params=_params(("arbitrary", "arbitrary")),
        name="ffn_seq" if group is None else "ffn_grouped",
    )(x1, co, wco, gf, wup, cw, cb, wdn, gfin, *hist)


def _rope_tables(pos):
    half = HD_DIFF // 2
    inv = ROPE_THETA ** (-jnp.arange(half, dtype=F32) * (2.0 / HD_DIFF))
    ang = pos.astype(F32)[:, None] * inv[None, :]
    cos = jnp.cos(ang)
    sin = jnp.sin(ang)
    return (jnp.concatenate([cos] * 4, axis=1), jnp.concatenate([-sin, sin] * 2, axis=1))


def kernel(x_prompt, x_sample, cache_fox_k, cache_fox_v, cache_fox_logf, cache_diff_k, cache_diff_v, cache_mem_k, cache_mem_v, state_conv, page_table, mem_prompt, g_mix, w_in, b_fgate, lambda_q1, lambda_k1, lambda_q2, lambda_k2, g_subln, w_br_fox, w_br_diff, w_mix_out, g_cross, g_mem, w_cq, w_ck, w_cv, w_co, g_ffn, w_up, conv_w, conv_b, w_down, g_final):
    bp, seq, _ = x_prompt.shape
    bs, dec, _ = x_sample.shape
    n_pool = cache_fox_k.shape[1]
    n_pages = page_table.shape[1]
    past = n_pages * PAGE_SIZE
    l = 0

    w = w_in[l]
    c_fl = 3 * W_FOX
    w_cat = jnp.concatenate([w[:, :c_fl], w[:, c_fl + H_FOX:]], axis=1).astype(BF16)
    w_fl = jnp.pad(w[:, c_fl:c_fl + H_FOX], ((0, 0), (0, LANES - H_FOX))).astype(BF16)
    b_fl = jnp.pad(b_fgate[l], (0, LANES - H_FOX))[None, :]
    row = lambda v: v[None, :]
    lam = jnp.stack([lambda_q1[l], lambda_k1[l], lambda_q2[l], lambda_k2[l]])
    gs = row(g_subln[l])
    wbf, wbd, wmo = w_br_fox[l].astype(BF16), w_br_diff[l].astype(BF16), w_mix_out[l].astype(BF16)
    wcq, wck, wcv, wco = (w_cq[l].astype(BF16), w_ck[l].astype(BF16), w_cv[l].astype(BF16),
                          w_co[l].astype(BF16))
    wup, wdn = w_up[l].astype(BF16), w_down[l].astype(BF16)

    def tail(x1, co, hist, *, tm, batch, group):
        return _ffn(x1, co, wco, row(g_ffn[l]), wup, conv_w[l], row(conv_b[l]), wdn, row(g_final),
                    hist, tm=tm, batch=batch, group=group)

    tp = bp * seq
    xp = x_prompt.reshape(tp, D_MODEL)
    cos_p, sin_p = _rope_tables(jnp.arange(seq, dtype=jnp.int32))
    tm_p = 512
    (fq, fk, fv, fkb, fvb, lf, dq, dk, dv, dkb, dvb, ga, gb, ckt) = _in_proj(
        xp, row(g_mix[l]), w_cat, w_fl, b_fl, cos_p, sin_p, tm=tm_p, tiles_per_seq=seq // tm_p)
    o_fox = _prompt_attn(fq, fkb, fvb, (ckt,), fox=True, batch=bp, seq=seq, tq=256, tk=1024)
    o_diff = _prompt_attn(dq, dkb, dvb, (lam, gs), fox=False, batch=bp, seq=seq, tq=256, tk=1024)

    mem_len = mem_prompt.shape[1]
    mk, mv, mkb, mvb = _mem_kv(mem_prompt.reshape(bp * mem_len, D_MODEL), row(g_mem[l]), wck, wcv, tm=256)
    x1, cq = _merge(xp, o_fox, o_diff, ga, gb, wbf, wbd, wmo, row(g_cross[l]), wcq, tm=512, cq_dtype=BF16)
    co = _cross_attn(cq, mkb.reshape(bp, mem_len, D_MODEL), mvb.reshape(bp, mem_len, D_MODEL), tq=512,
                     out_dtype=BF16, tiled=False)
    y_p, p_conv = tail(x1, co, (), tm=256, batch=bp, group=None)

    ts = bs * dec
    xs = x_sample.reshape(ts, D_MODEL)
    tm_s = 256
    pos_s = past + (jnp.arange(tm_s, dtype=jnp.int32) % dec)
    cos_s, sin_s = _rope_tables(pos_s)
    (sfq, sfk, sfv, sfkb, sfvb, slf, sdq, sdk, sdv, sdkb, sdvb, sga, sgb) = _in_proj(
        xs, row(g_mix[l]), w_cat, w_fl, b_fl, cos_s, sin_s, tm=tm_s, tiles_per_seq=None)

    def tokens_minor(c):
        c = c[l]
        return jnp.moveaxis(c, 1, -1).reshape(n_pool, -1, PAGE_SIZE)

    lf_new_page = jnp.pad(jnp.swapaxes(slf.reshape(bs, dec, H_FOX), 1, 2),
                          ((0, 0), (0, 0), (0, PAGE_SIZE - dec)))
    ck_past, ck_new = _page_cum(page_table, tokens_minor(cache_fox_logf), lf_new_page, pages=32)
    b3 = lambda a: a.reshape(bs, dec, a.shape[-1])
    so_fox = _paged_attn(page_table, b3(sfq), tokens_minor(cache_fox_k), tokens_minor(cache_fox_v),
                         b3(sfkb), b3(sfvb), (ck_past, ck_new), fox=True, pages=16)
    diff_v = cache_diff_v[l].reshape(n_pool, PAGE_SIZE * H_DIFF, 2 * HD_DIFF)
    so_diff = _paged_attn(page_table, b3(sdq), tokens_minor(cache_diff_k), diff_v,
                          b3(sdkb), b3(sdvb), (lam, gs), fox=False, pages=16)

    sx1, scq = _merge(xs, so_fox.reshape(ts, W_FOX), so_diff.reshape(ts, W_DIFF), sga, sgb, wbf, wbd, wmo,
                      row(g_cross[l]), wcq, tm=tm_s, cq_dtype=F32)
    def smem(c):
        c = c[l].reshape(bs, c.shape[2], H_MEM, HD_MEM // LANES, LANES)
        return jnp.swapaxes(c, 2, 3).reshape(bs, -1, LANES)

    sco = _cross_attn(scq, smem(cache_mem_k), smem(cache_mem_v), tq=dec, out_dtype=F32, tiled=True)
    st = state_conv[l]
    e2 = jnp.pad(st, ((0, 0), (0, dec - 2), (0, 0))).reshape(ts, D_FF)
    e1 = jnp.pad(st[:, 1:], ((0, 0), (0, dec - 1), (0, 0))).reshape(ts, D_FF)
    y_s, s_conv = tail(sx1, sco, (e1, e2), tm=tm_s, batch=bs, group=dec)

    d1 = lambda a, *shape: a.reshape((1,) + shape)
    tok = lambda a, *feat: jnp.moveaxis(a.reshape((1, bp) + feat + (seq,)), -1, 2)
    return (y_p.reshape(bp, seq, D_MODEL), y_s.reshape(bs, dec, D_MODEL),
            tok(fk, H_FOX, HD_FOX), tok(fv, H_FOX, HD_FOX), d1(lf, bp, seq, H_FOX),
            tok(dk, H_DIFF, 2, HD_DIFF), d1(dv, bp, seq, H_DIFF, 2 * HD_DIFF),
            d1(mk, bp, mem_len, H_MEM, HD_MEM), d1(mv, bp, mem_len, H_MEM, HD_MEM),
            d1(p_conv, bp, CONV_W - 1, D_FF),
            d1(sfk, bs, dec, H_FOX, HD_FOX), d1(sfv, bs, dec, H_FOX, HD_FOX), d1(slf, bs, dec, H_FOX),
            d1(sdk, bs, dec, H_DIFF, 2, HD_DIFF), d1(sdv, bs, dec, H_DIFF, 2 * HD_DIFF),
            d1(s_conv, bs, CONV_W - 1, D_FF))
```

```python
import functools
import math

import jax
import jax.numpy as jnp
from jax import lax
from jax.experimental import pallas as pl
from jax.experimental.pallas import tpu as pltpu

F32 = jnp.float32
BF16 = jnp.bfloat16

D_MODEL = 1024
H_FOX = 8
HD_FOX = 64
W_FOX = H_FOX * HD_FOX
H_DIFF = 4
HD_DIFF = 64
W_DIFF = H_DIFF * 2 * HD_DIFF
H_MEM = 4
HD_MEM = D_MODEL // H_MEM
D_FF = 2816
CONV_W = 3
PAGE_SIZE = 128
ROPE_THETA = 10000.0
EPS = 1e-6
LAM_INIT = 0.8 - 0.6 * math.exp(-0.3 * 0)
LOG2E = math.log2(math.e)

LANES = 128
V7X_VMEM_BYTES = 64 * 1024 * 1024
VMEM_LIMIT = V7X_VMEM_BYTES - 8 * 1024 * 1024
NEG_INF = float("-inf")

NT_DIMS = (((1,), (1,)), ((), ()))
TN_DIMS = (((0,), (0,)), ((), ()))


def _rms(x, g):
    r = lax.rsqrt(jnp.mean(x * x, axis=-1, keepdims=True) + EPS)
    return x * r * g


def _resident(shape):
    nd = len(shape)
    return pl.BlockSpec(shape, lambda *_: (0,) * nd, pipeline_mode=pl.Buffered(1))


def _params(semantics):
    return pltpu.CompilerParams(dimension_semantics=semantics, vmem_limit_bytes=VMEM_LIMIT)


def _rope(x, cos, sin_signed):
    n = x.shape[1]
    lane = lax.broadcasted_iota(jnp.int32, x.shape, 1)
    partner = jnp.where(lane % HD_DIFF < HD_DIFF // 2,
                        pltpu.roll(x, n - HD_DIFF // 2, 1),
                        pltpu.roll(x, HD_DIFF // 2, 1))
    reps = n // LANES
    c = jnp.concatenate([cos] * reps, axis=1)
    s = jnp.concatenate([sin_signed] * reps, axis=1)
    return x * c + partner * s


def _in_proj_kernel(x_ref, g_ref, w_ref, wfl_ref, bfl_ref, cos_ref, sin_ref,
                    fq_ref, fk_ref, fv_ref, fkb_ref, fvb_ref, lf_ref,
                    dq_ref, dk_ref, dv_ref, dkb_ref, dvb_ref, ga_ref, gb_ref,
                    *rest, tiles_per_seq):
    h = _rms(x_ref[...], g_ref[...]).astype(BF16)

    def proj(c0, n):
        return jnp.dot(h, w_ref[:, c0:c0 + n], preferred_element_type=F32)

    def store_kv(ref, val):
        if tiles_per_seq is None:
            ref[...] = val
        else:
            ref[0] = val.T

    fq_ref[...] = (proj(0, W_FOX) * (HD_FOX ** -0.5 * LOG2E)).astype(BF16)
    fk = proj(W_FOX, W_FOX)
    store_kv(fk_ref, fk)
    fkb_ref[...] = fk.astype(BF16)
    fv = proj(2 * W_FOX, W_FOX)
    store_kv(fv_ref, fv)
    fvb_ref[...] = fv.astype(BF16)

    u = jnp.dot(h, wfl_ref[...], preferred_element_type=F32) + bfl_ref[...]
    lf = jnp.minimum(u, 0.0) - jnp.log1p(jnp.exp(-jnp.abs(u)))
    lf_ref[...] = lf[:, :H_FOX]

    cos = cos_ref[...]
    sin = sin_ref[...]
    c0 = 3 * W_FOX
    dq_ref[...] = (_rope(proj(c0, W_DIFF), cos, sin) * (HD_DIFF ** -0.5 * LOG2E)).astype(BF16)
    dk = _rope(proj(c0 + W_DIFF, W_DIFF), cos, sin)
    store_kv(dk_ref, dk)
    dkb_ref[...] = dk.astype(BF16)
    dv = proj(c0 + 2 * W_DIFF, W_DIFF)
    dv_ref[...] = dv
    dvb_ref[...] = dv.astype(BF16)
    c0 += 3 * W_DIFF
    ga_ref[...] = jax.nn.sigmoid(proj(c0, D_MODEL)).astype(BF16)
    gb_ref[...] = jax.nn.sigmoid(proj(c0 + D_MODEL, D_MODEL)).astype(BF16)

    if tiles_per_seq is not None:
        ck_ref, carry_ref = rest
        tm = lf.shape[0]

        @pl.when(pl.program_id(0) % tiles_per_seq == 0)
        def _():
            carry_ref[...] = jnp.zeros_like(carry_ref)

        row = lax.broadcasted_iota(jnp.int32, (tm, tm), 0)
        col = lax.broadcasted_iota(jnp.int32, (tm, tm), 1)
        tri = (row >= col).astype(F32)
        cum = jnp.dot(tri, lf, precision=lax.Precision.HIGHEST,
                      preferred_element_type=F32) + carry_ref[...]
        carry_ref[...] = cum[tm - 1:tm, :]
        ck_ref[...] = cum.T[:H_FOX, :]


def _in_proj(x, g, w_cat, w_fl, b_fl, cos, sin, *, tm, tiles_per_seq):
    t = x.shape[0]
    n_tiles = t // tm
    with_cum = tiles_per_seq is not None
    table_tiles = cos.shape[0] // tm
    row = lambda i: (i, 0)
    tab = lambda i: (i % table_tiles, 0)
    wide = lambda n, dt: jax.ShapeDtypeStruct((t, n), dt)
    blk = lambda n: pl.BlockSpec((tm, n), row)
    if with_cum:
        kv = lambda n: jax.ShapeDtypeStruct((n_tiles // tiles_per_seq, n, tiles_per_seq * tm), F32)
        kv_blk = lambda n: pl.BlockSpec((1, n, tm), lambda i: (i // tiles_per_seq, 0, i % tiles_per_seq))
    else:
        kv = lambda n: wide(n, F32)
        kv_blk = blk
    out_shape = [wide(W_FOX, BF16), kv(W_FOX), kv(W_FOX), wide(W_FOX, BF16),
                 wide(W_FOX, BF16), wide(H_FOX, F32),
                 wide(W_DIFF, BF16), kv(W_DIFF), wide(W_DIFF, F32), wide(W_DIFF, BF16),
                 wide(W_DIFF, BF16), wide(D_MODEL, BF16), wide(D_MODEL, BF16)]
    out_specs = ([blk(W_FOX), kv_blk(W_FOX), kv_blk(W_FOX), blk(W_FOX), blk(W_FOX), blk(H_FOX),
                  blk(W_DIFF), kv_blk(W_DIFF)] + [blk(W_DIFF)] * 3 + [blk(D_MODEL)] * 2)
    scratch = []
    if with_cum:
        out_shape.append(jax.ShapeDtypeStruct((H_FOX, t), F32))
        out_specs.append(pl.BlockSpec((H_FOX, tm), lambda i: (0, i)))
        scratch.append(pltpu.VMEM((1, LANES), F32))
    return pl.pallas_call(
        functools.partial(_in_proj_kernel, tiles_per_seq=tiles_per_seq),
        grid=(n_tiles,),
        in_specs=[blk(D_MODEL), _resident(g.shape), _resident(w_cat.shape),
                  _resident(w_fl.shape), _resident(b_fl.shape),
                  pl.BlockSpec((tm, LANES), tab), pl.BlockSpec((tm, LANES), tab)],
        out_specs=out_specs,
        out_shape=out_shape,
        scratch_shapes=scratch,
        compiler_params=_params(("arbitrary",)),
        name="in_proj_cum" if with_cum else "in_proj",
    )(x, g, w_cat, w_fl, b_fl, cos, sin)


def _lambda(lam_ref):
    lp = lam_ref[...]
    d1 = jnp.sum(lp[0:1] * lp[1:2], axis=1, keepdims=True)
    d2 = jnp.sum(lp[2:3] * lp[3:4], axis=1, keepdims=True)
    return jnp.exp(d1) - jnp.exp(d2) + LAM_INIT


def _prompt_attn_kernel(qt_ref, kt_ref, diag_ref, q_ref, k_ref, v_ref, *rest, fox):
    if fox:
        ck_ref, o_ref, m_ref, l_ref, acc_ref = rest
    else:
        lam_ref, gs_ref, o_ref, m_ref, l_ref, acc_ref = rest
    t = pl.program_id(1)
    ki = kt_ref[t]
    tq = q_ref.shape[0]
    tk = k_ref.shape[0]
    pairs = q_ref.shape[1] // LANES
    low = lax.broadcasted_iota(jnp.int32, (tq, LANES), 1) < HD_FOX

    @pl.when(ki == 0)
    def _():
        m_ref[...] = jnp.full_like(m_ref, NEG_INF)
        l_ref[...] = jnp.zeros_like(l_ref)
        acc_ref[...] = jnp.zeros_like(acc_ref)

    def step(ncols, masked):
        if masked:
            row = lax.broadcasted_iota(jnp.int32, (tq, tq), 0)
            col = lax.broadcasted_iota(jnp.int32, (tq, tq), 1)
            visible = col <= row
        logits = []
        for pr in range(pairs):
            cols = slice(pr * LANES, (pr + 1) * LANES)
            q = q_ref[:, cols]
            k = k_ref[0:ncols, cols]
            for half in range(2):
                qh = jnp.where(low if half == 0 else jnp.logical_not(low), q, jnp.zeros_like(q))
                logits.append(lax.dot_general(qh, k, NT_DIMS, preferred_element_type=F32))
        probs = []
        for idx, s in enumerate(logits):
            if fox:
                s = s - ck_ref[idx:idx + 1, 0:ncols] * LOG2E
            if masked:
                edge = jnp.where(visible, s[:, ncols - tq:], NEG_INF)
                s = edge if ncols == tq else jnp.concatenate([s[:, :ncols - tq], edge], axis=1)
            m_prev = m_ref[idx]
            m_new = jnp.maximum(m_prev, jnp.max(s, axis=1, keepdims=True))
            alpha = jnp.exp2(m_prev - m_new)
            m_ref[idx] = m_new
            probs.append((alpha, jnp.exp2(s - m_new).astype(BF16)))
        ones = jnp.ones((ncols, LANES), BF16)
        for idx, (alpha, p) in enumerate(probs):
            pr = idx // 2
            v = jnp.concatenate([v_ref[0:ncols, pr * LANES:(pr + 1) * LANES], ones], axis=1)
            pv = jnp.dot(p, v, preferred_element_type=F32)
            acc_ref[idx] = alpha * acc_ref[idx] + pv[:, :LANES]
            l_ref[idx] = alpha * l_ref[idx] + pv[:, LANES:]

    def finalize():
        lam = None if fox else _lambda(lam_ref)
        for pr in range(pairs):
            cols = slice(pr * LANES, (pr + 1) * LANES)
            o0 = acc_ref[2 * pr] / l_ref[2 * pr]
            o1 = acc_ref[2 * pr + 1] / l_ref[2 * pr + 1]
            if fox:
                o_ref[:, cols] = jnp.where(low, o0, o1).astype(o_ref.dtype)
            else:
                o = o0 - lam * o1
                o_ref[:, cols] = (_rms(o, gs_ref[...]) * (1.0 - LAM_INIT)).astype(o_ref.dtype)

    @pl.when(diag_ref[t] == 0)
    def _():
        step(tk, False)

    for j in range(1, tk // tq + 1):
        @pl.when(diag_ref[t] == j)
        def _(j=j):
            step(j * tq, True)
            finalize()


def _prompt_attn(q, k, v, extra, *, fox, batch, seq, tq, tk):
    nq = seq // tq
    nk = seq // tk
    width = q.shape[1]
    maps = width // HD_FOX
    sched = [(i, j) for i in range(nq) for j in range((i * tq) // tk + 1)]
    qt = jnp.asarray([i for i, _ in sched], jnp.int32)
    kt = jnp.asarray([j for _, j in sched], jnp.int32)
    diag = jnp.asarray([((i * tq) % tk) // tq + 1 if j == (i * tq) // tk else 0 for i, j in sched],
                       jnp.int32)
    qmap = lambda b, t, qt, kt, diag: (b * nq + qt[t], 0)
    kmap = lambda b, t, qt, kt, diag: (b * nk + kt[t], 0)
    in_specs = [pl.BlockSpec((tq, width), qmap), pl.BlockSpec((tk, width), kmap),
                pl.BlockSpec((tk, width), kmap)]
    if fox:
        in_specs.append(pl.BlockSpec((maps, tk), lambda b, t, qt, kt, diag: (0, b * nk + kt[t])))
    else:
        in_specs += [pl.BlockSpec(e.shape, lambda b, t, qt, kt, diag: (0, 0)) for e in extra]
    return pl.pallas_call(
        functools.partial(_prompt_attn_kernel, fox=fox),
        grid_spec=pltpu.PrefetchScalarGridSpec(
            num_scalar_prefetch=3,
            grid=(batch, len(sched)),
            in_specs=in_specs,
            out_specs=pl.BlockSpec((tq, width), qmap),
            scratch_shapes=[pltpu.VMEM((maps, tq, 1), F32), pltpu.VMEM((maps, tq, LANES), F32),
                            pltpu.VMEM((maps, tq, LANES), F32)]),
        out_shape=jax.ShapeDtypeStruct(q.shape, BF16),
        compiler_params=_params(("arbitrary", "arbitrary")),
        name="prompt_attn_fox" if fox else "prompt_attn_diff",
    )(qt, kt, diag, q, k, v, *extra)


def _page_cum_kernel(pt_ref, *refs, pages):
    lf_refs = refs[:pages]
    new_ref, ck_ref, cknew_ref, carry_ref = refs[pages:]
    c = pl.program_id(1)

    @pl.when(c == 0)
    def _():
        carry_ref[...] = jnp.zeros_like(carry_ref)

    row = lax.broadcasted_iota(jnp.int32, (PAGE_SIZE, PAGE_SIZE), 0)
    col = lax.broadcasted_iota(jnp.int32, (PAGE_SIZE, PAGE_SIZE), 1)
    tri = (row <= col).astype(F32)

    def page_cum(x):
        return jnp.dot(x, tri, precision=lax.Precision.HIGHEST, preferred_element_type=F32)

    within = page_cum(jnp.concatenate([r[0] for r in lf_refs], axis=0))
    carry = carry_ref[...]
    for j in range(pages):
        cj = within[j * H_FOX:(j + 1) * H_FOX]
        ck_ref[0, :, j * PAGE_SIZE:(j + 1) * PAGE_SIZE] = cj + carry
        carry = carry + cj[:, PAGE_SIZE - 1:PAGE_SIZE]
    carry_ref[...] = carry

    @pl.when(c == pl.num_programs(1) - 1)
    def _():
        cknew_ref[0] = page_cum(new_ref[0]) + carry


def _page_cum(page_table, cache_lf, lf_new_page, *, pages):
    b, n_pages = page_table.shape
    steps = n_pages // pages
    in_specs = [pl.BlockSpec((1, H_FOX, PAGE_SIZE),
                             functools.partial(lambda b_, c, pt, j: (pt[b_, c * pages + j], 0, 0), j=j))
                for j in range(pages)]
    in_specs.append(pl.BlockSpec((1, H_FOX, PAGE_SIZE), lambda b_, c, pt: (b_, 0, 0)))
    return pl.pallas_call(
        functools.partial(_page_cum_kernel, pages=pages),
        grid_spec=pltpu.PrefetchScalarGridSpec(
            num_scalar_prefetch=1,
            grid=(b, steps),
            in_specs=in_specs,
            out_specs=[pl.BlockSpec((1, H_FOX, pages * PAGE_SIZE), lambda b_, c, pt: (b_, 0, c)),
                       pl.BlockSpec((1, H_FOX, PAGE_SIZE), lambda b_, c, pt: (b_, 0, 0))],
            scratch_shapes=[pltpu.VMEM((H_FOX, 1), F32)]),
        out_shape=[jax.ShapeDtypeStruct((b, H_FOX, n_pages * PAGE_SIZE), F32),
                   jax.ShapeDtypeStruct((b, H_FOX, PAGE_SIZE), F32)],
        compiler_params=_params(("arbitrary", "arbitrary")),
        name="page_cum",
    )(page_table, *([cache_lf] * pages), lf_new_page)


def _block_diag_q(q):
    n_groups = q.shape[1] // HD_FOX
    rows = q.shape[0]
    qq = jnp.concatenate([q] * n_groups, axis=0)
    r = lax.broadcasted_iota(jnp.int32, qq.shape, 0)
    c = lax.broadcasted_iota(jnp.int32, qq.shape, 1)
    return jnp.where(r // rows == c // HD_FOX, qq, jnp.zeros_like(qq))


def _paged_attn_kernel(pt_ref, *refs, fox, pages):
    q_ref = refs[0]
    kt_refs = refs[1:1 + pages]
    v_refs = refs[1 + pages:1 + 2 * pages]
    rest = refs[1 + 2 * pages:]
    knew_ref, vnew_ref = rest[:2]
    if fox:
        ck_ref, cknew_ref, o_ref, m_ref, l_ref, acc_ref = rest[2:]
    else:
        lam_ref, gs_ref, o_ref, m_ref, l_ref, acc_ref = rest[2:]
    c = pl.program_id(1)
    n_q = q_ref.shape[1]
    width = q_ref.shape[2]
    groups = width // HD_FOX
    head_rows = 2 * n_q
    head_w = 2 * HD_DIFF

    @pl.when(c == 0)
    def _():
        m_ref[...] = jnp.full_like(m_ref, NEG_INF)
        l_ref[...] = jnp.zeros_like(l_ref)
        acc_ref[...] = jnp.zeros_like(acc_ref)

    qbd = _block_diag_q(q_ref[0])

    def bias(s, ck):
        ck = ck * LOG2E
        return jnp.concatenate(
            [s[g * n_q:(g + 1) * n_q] - ck[g:g + 1] for g in range(groups)], axis=0)

    def softmax_step(s):
        m_prev = m_ref[...]
        m_new = jnp.maximum(m_prev, jnp.max(s, axis=1, keepdims=True))
        alpha = jnp.exp2(m_prev - m_new)
        p = jnp.exp2(s - m_new)
        l_ref[...] = alpha * l_ref[...] + jnp.sum(p, axis=1, keepdims=True)
        m_ref[...] = m_new
        return alpha, p.astype(BF16)

    def pair(refs_, j, axis, index=()):
        return jnp.concatenate([refs_[j][(0,) + index].astype(BF16),
                                refs_[j + 1][(0,) + index].astype(BF16)], axis=axis)

    def accumulate(alpha, terms):
        for rows, prods in terms:
            pv = None
            for lhs, rhs, dims in prods:
                d = lax.dot_general(lhs, rhs, dims, preferred_element_type=F32)
                pv = d if pv is None else pv + d
            acc_ref[rows, :] = alpha[rows] * acc_ref[rows, :] + pv

    nn = (((1,), (0,)), ((), ()))
    all_rows = slice(0, groups * n_q)
    two = 2 * PAGE_SIZE
    s = jnp.concatenate(
        [jnp.dot(qbd, pair(kt_refs, j, 1), preferred_element_type=F32)
         for j in range(0, pages, 2)], axis=1)
    if fox:
        s = bias(s, ck_ref[0])
    alpha, p = softmax_step(s)
    if fox:
        accumulate(alpha, [(all_rows, [(p[:, j * PAGE_SIZE:j * PAGE_SIZE + two], pair(v_refs, j, 1), NT_DIMS)
                                       for j in range(0, pages, 2)])])
    else:
        accumulate(alpha, [
            (slice(hd * head_rows, (hd + 1) * head_rows),
             [(p[hd * head_rows:(hd + 1) * head_rows, j * PAGE_SIZE:j * PAGE_SIZE + two],
               pair(v_refs, j, 0, (pl.ds(hd, PAGE_SIZE, stride=H_DIFF), slice(None))), nn)
              for j in range(0, pages, 2)])
            for hd in range(H_DIFF)])

    @pl.when(c == pl.num_programs(1) - 1)
    def _():
        pad = jnp.zeros((PAGE_SIZE - n_q, width), BF16)
        kn = jnp.concatenate([knew_ref[0], pad], axis=0)
        vn = jnp.concatenate([vnew_ref[0], pad], axis=0)
        sn = lax.dot_general(qbd, kn, NT_DIMS, preferred_element_type=F32)
        if fox:
            sn = bias(sn, cknew_ref[0])
        r = lax.broadcasted_iota(jnp.int32, sn.shape, 0)
        col = lax.broadcasted_iota(jnp.int32, sn.shape, 1)
        sn = jnp.where(col <= r % n_q, sn, NEG_INF)
        alpha_n, pn = softmax_step(sn)
        if fox:
            accumulate(alpha_n, [(all_rows, [(pn, vn, nn)])])
        else:
            accumulate(alpha_n, [
                (slice(hd * head_rows, (hd + 1) * head_rows),
                 [(pn[hd * head_rows:(hd + 1) * head_rows], vn[:, hd * head_w:(hd + 1) * head_w], nn)])
                for hd in range(H_DIFF)])

        o_full = acc_ref[...] / l_ref[...]
        if fox:
            lane = lax.broadcasted_iota(jnp.int32, (n_q, width), 1)
            o = jnp.zeros((n_q, width), F32)
            for g in range(groups):
                o = o + jnp.where(lane // HD_FOX == g, o_full[g * n_q:(g + 1) * n_q], 0.0)
            o_ref[0] = o.astype(o_ref.dtype)
        else:
            lam = _lambda(lam_ref)
            heads = []
            for hd in range(H_DIFF):
                o0 = o_full[hd * head_rows:hd * head_rows + n_q]
                o1 = o_full[hd * head_rows + n_q:(hd + 1) * head_rows]
                heads.append(_rms(o0 - lam * o1, gs_ref[...]))
            o_ref[0] = (jnp.concatenate(heads, axis=1) * (1.0 - LAM_INIT)).astype(o_ref.dtype)


def _paged_attn(page_table, q, cache_kt, cache_v, k_new, v_new, extra, *, fox, pages):
    b, n_q, width = q.shape
    n_pages = page_table.shape[1]
    steps = n_pages // pages
    page_spec = lambda j: pl.BlockSpec(
        (1,) + cache_kt.shape[1:], lambda b_, c, pt: (pt[b_, c * pages + j], 0, 0))
    per_b = pl.BlockSpec((1, n_q, width), lambda b_, c, pt: (b_, 0, 0))
    in_specs = [per_b] + [page_spec(j) for j in range(pages)] * 2 + [per_b, per_b]
    if fox:
        in_specs += [pl.BlockSpec((1, H_FOX, pages * PAGE_SIZE), lambda b_, c, pt: (b_, 0, c)),
                     pl.BlockSpec((1, H_FOX, PAGE_SIZE), lambda b_, c, pt: (b_, 0, 0))]
    else:
        in_specs += [pl.BlockSpec(e.shape, lambda b_, c, pt: (0, 0)) for e in extra]
    rows = n_q * (width // HD_FOX)
    acc_w = width if fox else 2 * HD_DIFF
    return pl.pallas_call(
        functools.partial(_paged_attn_kernel, fox=fox, pages=pages),
        grid_spec=pltpu.PrefetchScalarGridSpec(
            num_scalar_prefetch=1,
            grid=(b, steps),
            in_specs=in_specs,
            out_specs=per_b,
            scratch_shapes=[pltpu.VMEM((rows, 1), F32), pltpu.VMEM((rows, 1), F32),
                            pltpu.VMEM((rows, acc_w), F32)]),
        out_shape=jax.ShapeDtypeStruct(q.shape, BF16),
        compiler_params=_params(("arbitrary", "arbitrary")),
        name="paged_attn_fox" if fox else "paged_attn_diff",
    )(page_table, q, *([cache_kt] * pages), *([cache_v] * pages), k_new, v_new, *extra)


def _mem_kv_kernel(m_ref, g_ref, wk_ref, wv_ref, k_ref, v_ref, kb_ref, vb_ref):
    mn = _rms(m_ref[...], g_ref[...]).astype(BF16)
    k = jnp.dot(mn, wk_ref[...], preferred_element_type=F32)
    v = jnp.dot(mn, wv_ref[...], preferred_element_type=F32)
    k_ref[...] = k
    v_ref[...] = v
    kb_ref[...] = k.astype(BF16)
    vb_ref[...] = v.astype(BF16)


def _mem_kv(mem, g, wk, wv, *, tm):
    t = mem.shape[0]
    blk = pl.BlockSpec((tm, D_MODEL), lambda i: (i, 0))
    return pl.pallas_call(
        _mem_kv_kernel,
        grid=(t // tm,),
        in_specs=[blk, _resident(g.shape), _resident(wk.shape), _resident(wv.shape)],
        out_specs=[blk] * 4,
        out_shape=[jax.ShapeDtypeStruct(mem.shape, F32)] * 2 + [jax.ShapeDtypeStruct(mem.shape, BF16)] * 2,
        compiler_params=_params(("arbitrary",)),
        name="mem_kv",
    )(mem, g, wk, wv)


def _merge_kernel(x_ref, of_ref, od_ref, ga_ref, gb_ref, wbf_ref, wbd_ref, wmo_ref,
                  gc_ref, wcq_ref, x1_ref, cq_ref):
    merged = (ga_ref[...].astype(F32) * jnp.dot(of_ref[...], wbf_ref[...], preferred_element_type=F32)
              + gb_ref[...].astype(F32) * jnp.dot(od_ref[...], wbd_ref[...], preferred_element_type=F32))
    x1 = x_ref[...] + jnp.dot(merged.astype(BF16), wmo_ref[...], preferred_element_type=F32)
    x1_ref[...] = x1
    hc = _rms(x1, gc_ref[...]).astype(BF16)
    cq = jnp.dot(hc, wcq_ref[...], preferred_element_type=F32) * HD_MEM ** -0.5
    cq_ref[...] = cq.astype(cq_ref.dtype)


def _merge(x, of, od, ga, gb, wbf, wbd, wmo, gc, wcq, *, tm, cq_dtype):
    t = x.shape[0]
    row = lambda i: (i, 0)
    blk = lambda n: pl.BlockSpec((tm, n), row)
    return pl.pallas_call(
        _merge_kernel,
        grid=(t // tm,),
        in_specs=[blk(D_MODEL), blk(W_FOX), blk(W_DIFF), blk(D_MODEL), blk(D_MODEL),
                  _resident(wbf.shape), _resident(wbd.shape), _resident(wmo.shape),
                  _resident(gc.shape), _resident(wcq.shape)],
        out_specs=[blk(D_MODEL), blk(D_MODEL)],
        out_shape=[jax.ShapeDtypeStruct(x.shape, F32), jax.ShapeDtypeStruct(x.shape, cq_dtype)],
        compiler_params=_params(("arbitrary",)),
        name="merge",
    )(x, of, od, ga, gb, wbf, wbd, wmo, gc, wcq)


def _cross_attn_kernel(q_ref, k_ref, v_ref, o_ref, *, tiled):
    halves = HD_MEM // LANES
    for hd in range(H_MEM):
        sl = slice(hd * HD_MEM, (hd + 1) * HD_MEM)
        if tiled:
            m = k_ref.shape[1] // (halves * H_MEM)
            rows = lambda c: pl.ds(c * H_MEM + hd, m, stride=halves * H_MEM)
            k = jnp.concatenate([k_ref[0, rows(c), :] for c in range(halves)], axis=1)
            v = jnp.concatenate([v_ref[0, rows(c), :] for c in range(halves)], axis=1)
        else:
            k = k_ref[0, :, sl]
            v = v_ref[0, :, sl]
        q = q_ref[:, sl].astype(k.dtype)
        s = lax.dot_general(q, k, NT_DIMS, preferred_element_type=F32)
        p = jnp.exp(s - jnp.max(s, axis=1, keepdims=True))
        p = p / jnp.sum(p, axis=1, keepdims=True)
        o_ref[:, sl] = jnp.dot(p.astype(v.dtype), v, preferred_element_type=F32).astype(o_ref.dtype)


def _cross_attn(q, mem_k, mem_v, *, tq, out_dtype, tiled):
    t = q.shape[0]
    b = mem_k.shape[0]
    tiles = t // b // tq
    qmap = lambda b_, i: (b_ * tiles + i, 0)
    mmap = lambda b_, i: (b_, 0, 0)
    mem_spec = pl.BlockSpec((1,) + mem_k.shape[1:], mmap)
    return pl.pallas_call(
        functools.partial(_cross_attn_kernel, tiled=tiled),
        grid=(b, tiles),
        in_specs=[pl.BlockSpec((tq, D_MODEL), qmap), mem_spec, mem_spec],
        out_specs=pl.BlockSpec((tq, D_MODEL), qmap),
        out_shape=jax.ShapeDtypeStruct(q.shape, out_dtype),
        compiler_params=_params(("arbitrary", "arbitrary")),
        name="cross_attn",
    )(q, mem_k, mem_v)


FF_CHUNK = D_FF // 2


def _ffn_kernel(x1_ref, co_ref, wco_ref, gf_ref, wup_ref, cw_ref, cb_ref, wdn_ref, gfin_ref,
                *rest, group):
    if group is None:
        y_ref, cn_ref, carry_ref = rest
    else:
        e1_ref, e2_ref, y_ref, cn_ref = rest
    tm = x1_ref.shape[0]
    x2 = x1_ref[...] + jnp.dot(co_ref[...].astype(BF16), wco_ref[...], preferred_element_type=F32)
    hf = _rms(x2, gf_ref[...]).astype(BF16)
    pos = lax.broadcasted_iota(jnp.int32, (tm, 1), 0)
    if group is None:
        @pl.when(pl.program_id(1) == 0)
        def _():
            carry_ref[...] = jnp.zeros_like(carry_ref)
    else:
        pos = pos % group

    acc = jnp.zeros((tm, D_MODEL), F32)
    for c0 in range(0, D_FF, FF_CHUNK):
        cs = slice(c0, c0 + FF_CHUNK)
        a = jnp.dot(hf, wup_ref[:, cs], preferred_element_type=F32)
        b = jnp.dot(hf, wup_ref[:, D_FF + c0:D_FF + c0 + FF_CHUNK], preferred_element_type=F32)
        if group is None:
            h0 = carry_ref[0:1, cs]
            h1 = carry_ref[1:2, cs]
            a_m1 = jnp.where(pos == 0, h1, pltpu.roll(a, 1, 0))
            a_m2 = jnp.where(pos == 0, h0, jnp.where(pos == 1, h1, pltpu.roll(a, 2, 0)))
            tail = a[tm - (CONV_W - 1):, :]
            carry_ref[0:CONV_W - 1, cs] = tail
            cn_ref[0, :, cs] = tail
        else:
            a_m1 = jnp.where(pos == 0, e1_ref[:, cs], pltpu.roll(a, 1, 0))
            a_m2 = jnp.where(pos < 2, e2_ref[:, cs], pltpu.roll(a, 2, 0))
            cn_ref[:, :, cs] = a.reshape(tm // group, group, FF_CHUNK)[:, group - (CONV_W - 1):, :]
        ac = cw_ref[0:1, cs] * a_m2 + cw_ref[1:2, cs] * a_m1 + cw_ref[2:3, cs] * a + cb_ref[:, cs]
        gate = ac * jax.nn.sigmoid(ac) * b
        acc = acc + jnp.dot(gate.astype(BF16), wdn_ref[cs, :], preferred_element_type=F32)
    y_ref[...] = _rms(x2 + acc, gfin_ref[...])


def _ffn(x1, co, wco, gf, wup, cw, cb, wdn, gfin, hist, *, tm, batch, group):
    t = x1.shape[0]
    tiles = t // batch // tm if group is None else t // tm
    if group is None:
        grid = (batch, tiles)
        row = lambda b_, i: (b_ * tiles + i, 0)
        cn_spec = pl.BlockSpec((1, CONV_W - 1, D_FF), lambda b_, i: (b_, 0, 0))
        hist_specs = []
        scratch = [pltpu.VMEM((8, D_FF), F32)]
    else:
        grid = (1, tiles)
        row = lambda b_, i: (i, 0)
        cn_spec = pl.BlockSpec((tm // group, CONV_W - 1, D_FF), lambda b_, i: (i, 0, 0))
        hist_specs = [pl.BlockSpec((tm, D_FF), row)] * 2
        scratch = []
    blk = pl.BlockSpec((tm, D_MODEL), row)
    return pl.pallas_call(
        functools.partial(_ffn_kernel, group=group),
        grid=grid,
        in_specs=[blk, blk, _resident(wco.shape), _resident(gf.shape), _resident(wup.shape),
                  _resident(cw.shape), _resident(cb.shape), _resident(wdn.shape),
                  _resident(gfin.shape)] + hist_specs,
        out_specs=[blk, cn_spec],
        out_shape=[jax.ShapeDtypeStruct(x1.shape, F32),
                   jax.ShapeDtypeStruct((batch, CONV_W - 1, D_FF), F32)],
        scratch_shapes=scratch,
        compiler_params=_params(("arbitrary", "arbitrary")),
        name="ffn_seq" if group is None else "ffn_grouped",
    )(x1, co, wco, gf, wup, cw, cb, wdn, gfin, *hist)


def _rope_tables(pos):
    half = HD_DIFF // 2
    inv = ROPE_THETA ** (-jnp.arange(half, dtype=F32) * (2.0 / HD_DIFF))
    ang = pos.astype(F32)[:, None] * inv[None, :]
    cos = jnp.cos(ang)
    sin = jnp.sin(ang)
    return (jnp.concatenate([cos] * 4, axis=1), jnp.concatenate([-sin, sin] * 2, axis=1))


def kernel(x_prompt, x_sample, cache_fox_k, cache_fox_v, cache_fox_logf, cache_diff_k, cache_diff_v, cache_mem_k, cache_mem_v, state_conv, page_table, mem_prompt, g_mix, w_in, b_fgate, lambda_q1, lambda_k1, lambda_q2, lambda_k2, g_subln, w_br_fox, w_br_diff, w_mix_out, g_cross, g_mem, w_cq, w_ck, w_cv, w_co, g_ffn, w_up, conv_w, conv_b, w_down, g_final):
    bp, seq, _ = x_prompt.shape
    bs, dec, _ = x_sample.shape
    n_pool = cache_fox_k.shape[1]
    n_pages = page_table.shape[1]
    past = n_pages * PAGE_SIZE
    l = 0

    w = w_in[l]
    c_fl = 3 * W_FOX
    w_cat = jnp.concatenate([w[:, :c_fl], w[:, c_fl + H_FOX:]], axis=1).astype(BF16)
    w_fl = jnp.pad(w[:, c_fl:c_fl + H_FOX], ((0, 0), (0, LANES - H_FOX))).astype(BF16)
    b_fl = jnp.pad(b_fgate[l], (0, LANES - H_FOX))[None, :]
    row = lambda v: v[None, :]
    lam = jnp.stack([lambda_q1[l], lambda_k1[l], lambda_q2[l], lambda_k2[l]])
    gs = row(g_subln[l])
    wbf, wbd, wmo = w_br_fox[l].astype(BF16), w_br_diff[l].astype(BF16), w_mix_out[l].astype(BF16)
    wcq, wck, wcv, wco = (w_cq[l].astype(BF16), w_ck[l].astype(BF16), w_cv[l].astype(BF16),
                          w_co[l].astype(BF16))
    wup, wdn = w_up[l].astype(BF16), w_down[l].astype(BF16)

    def tail(x1, co, hist, *, tm, batch, group):
        return _ffn(x1, co, wco, row(g_ffn[l]), wup, conv_w[l], row(conv_b[l]), wdn, row(g_final),
                    hist, tm=tm, batch=batch, group=group)

    tp = bp * seq
    xp = x_prompt.reshape(tp, D_MODEL)
    cos_p, sin_p = _rope_tables(jnp.arange(seq, dtype=jnp.int32))
    tm_p = 512
    (fq, fk, fv, fkb, fvb, lf, dq, dk, dv, dkb, dvb, ga, gb, ckt) = _in_proj(
        xp, row(g_mix[l]), w_cat, w_fl, b_fl, cos_p, sin_p, tm=tm_p, tiles_per_seq=seq // tm_p)
    o_fox = _prompt_attn(fq, fkb, fvb, (ckt,), fox=True, batch=bp, seq=seq, tq=256, tk=1024)
    o_diff = _prompt_attn(dq, dkb, dvb, (lam, gs), fox=False, batch=bp, seq=seq, tq=256, tk=1024)

    mem_len = mem_prompt.shape[1]
    mk, mv, mkb, mvb = _mem_kv(mem_prompt.reshape(bp * mem_len, D_MODEL), row(g_mem[l]), wck, wcv, tm=256)
    x1, cq = _merge(xp, o_fox, o_diff, ga, gb, wbf, wbd, wmo, row(g_cross[l]), wcq, tm=512, cq_dtype=BF16)
    co = _cross_attn(cq, mkb.reshape(bp, mem_len, D_MODEL), mvb.reshape(bp, mem_len, D_MODEL), tq=512,
                     out_dtype=BF16, tiled=False)
    y_p, p_conv = tail(x1, co, (), tm=256, batch=bp, group=None)

    ts = bs * dec
    xs = x_sample.reshape(ts, D_MODEL)
    tm_s = 256
    pos_s = past + (jnp.arange(tm_s, dtype=jnp.int32) % dec)
    cos_s, sin_s = _rope_tables(pos_s)
    (sfq, sfk, sfv, sfkb, sfvb, slf, sdq, sdk, sdv, sdkb, sdvb, sga, sgb) = _in_proj(
        xs, row(g_mix[l]), w_cat, w_fl, b_fl, cos_s, sin_s, tm=tm_s, tiles_per_seq=None)

    def tokens_minor(c):
        c = c[l]
        return jnp.moveaxis(c, 1, -1).reshape(n_pool, -1, PAGE_SIZE)

    lf_new_page = jnp.pad(jnp.swapaxes(slf.reshape(bs, dec, H_FOX), 1, 2),
                          ((0, 0), (0, 0), (0, PAGE_SIZE - dec)))
    ck_past, ck_new = _page_cum(page_table, tokens_minor(cache_fox_logf), lf_new_page, pages=32)
    b3 = lambda a: a.reshape(bs, dec, a.shape[-1])
    so_fox = _paged_attn(page_table, b3(sfq), tokens_minor(cache_fox_k), tokens_minor(cache_fox_v),
                         b3(sfkb), b3(sfvb), (ck_past, ck_new), fox=True, pages=32)
    diff_v = cache_diff_v[l].reshape(n_pool, PAGE_SIZE * H_DIFF, 2 * HD_DIFF)
    so_diff = _paged_attn(page_table, b3(sdq), tokens_minor(cache_diff_k), diff_v,
                          b3(sdkb), b3(sdvb), (lam, gs), fox=False, pages=32)

    sx1, scq = _merge(xs, so_fox.reshape(ts, W_FOX), so_diff.reshape(ts, W_DIFF), sga, sgb, wbf, wbd, wmo,
                      row(g_cross[l]), wcq, tm=tm_s, cq_dtype=F32)
    def smem(c):
        c = c[l].reshape(bs, c.shape[2], H_MEM, HD_MEM // LANES, LANES)
        return jnp.swapaxes(c, 2, 3).reshape(bs, -1, LANES)

    sco = _cross_attn(scq, smem(cache_mem_k), smem(cache_mem_v), tq=dec, out_dtype=F32, tiled=True)
    st = state_conv[l]
    e2 = jnp.pad(st, ((0, 0), (0, dec - 2), (0, 0))).reshape(ts, D_FF)
    e1 = jnp.pad(st[:, 1:], ((0, 0), (0, dec - 1), (0, 0))).reshape(ts, D_FF)
    y_s, s_conv = tail(sx1, sco, (e1, e2), tm=tm_s, batch=bs, group=dec)

    d1 = lambda a, *shape: a.reshape((1,) + shape)
    tok = lambda a, *feat: jnp.moveaxis(a.reshape((1, bp) + feat + (seq,)), -1, 2)
    return (y_p.reshape(bp, seq, D_MODEL), y_s.reshape(bs, dec, D_MODEL),
            tok(fk, H_FOX, HD_FOX), tok(fv, H_FOX, HD_FOX), d1(lf, bp, seq, H_FOX),
            tok(dk, H_DIFF, 2, HD_DIFF), d1(dv, bp, seq, H_DIFF, 2 * HD_DIFF),
            d1(mk, bp, mem_len, H_MEM, HD_MEM), d1(mv, bp, mem_len, H_MEM, HD_MEM),
            d1(p_conv, bp, CONV_W - 1, D_FF),
            d1(sfk, bs, dec, H_FOX, HD_FOX), d1(sfv, bs, dec, H_FOX, HD_FOX), d1(slf, bs, dec, H_FOX),
            d1(sdk, bs, dec, H_DIFF, 2, HD_DIFF), d1(sdv, bs, dec, H_DIFF, 2 * HD_DIFF),
            d1(s_conv, bs, CONV_W - 1, D_FF))
```
